```python
import math
import jax, jax.numpy as jnp
from jax import lax
import numpy as np

D_MODEL = 1024
BATCH = 32
SEQ = 2048
DEPTH = 1

CTX_LEN = 256
GRID_W = 64
D_INNER = 2 * D_MODEL
D_SSM = D_INNER // 2
D_CONV = D_INNER - D_SSM
SSM_HEAD_DIM = 64
SSM_HEADS = D_SSM // SSM_HEAD_DIM
SSM_GROUPS = 2
SSM_STATE = 128
SSM_CONV_W = 5
SSM_CHUNK = 128
CONF_KERNEL = 31
CONF_CH_GROUP = 64
D_FF = ((int(8 * D_MODEL / 3) + 255) // 256) * 256
GN = SSM_GROUPS * SSM_STATE
OFF_Z = 0
OFF_X = OFF_Z + D_SSM
OFF_B = OFF_X + D_SSM
OFF_C = OFF_B + GN
OFF_DT = OFF_C + GN
OFF_GLU = OFF_DT + 2 * SSM_HEADS
D_IN_PROJ = OFF_GLU + 2 * D_CONV
N_MOD = 9
EPS = 1e-6

kernel_name = "hybrid_ssd_conformer_macaron_dit_block"


def rmsnorm(x, w):
    xf = x.astype(jnp.float32)
    y = xf * lax.rsqrt(jnp.mean(xf * xf, axis=-1, keepdims=True) + EPS)
    return (y * w.astype(jnp.float32)).astype(x.dtype)


def group_rmsnorm(x, w, groups):
    xf = x.astype(jnp.float32).reshape(*x.shape[:-1], groups, x.shape[-1] // groups)
    y = xf * lax.rsqrt(jnp.mean(xf * xf, axis=-1, keepdims=True) + EPS)
    return (y.reshape(x.shape) * w.astype(jnp.float32)).astype(x.dtype)


def layernorm(x, w, b):
    xf = x.astype(jnp.float32)
    mu = jnp.mean(xf, axis=-1, keepdims=True)
    var = jnp.mean(jnp.square(xf - mu), axis=-1, keepdims=True)
    y = (xf - mu) * lax.rsqrt(var + EPS)
    return (y * w.astype(jnp.float32) + b.astype(jnp.float32)).astype(x.dtype)


def modulate(h, shift, scale):
    return h * (1.0 + scale) + shift


def swiglu(h, w_gate, w_up, w_down):
    return (jax.nn.silu(h @ w_gate) * (h @ w_up)) @ w_down


def _flip(t):
    return jnp.flip(t, axis=1)


def dwconv1d(x, w, b):
    k = w.shape[0]
    pad = k // 2
    y = lax.conv_general_dilated(x, w[:, None, :], (1,), [(pad, pad)],
                                 dimension_numbers=("NWC", "WIO", "NWC"),
                                 feature_group_count=x.shape[-1])
    return y + b


def axial_dwconv(u, w, b, rows):
    bsz, seqlen, ch = u.shape
    k = w.shape[0]
    pad = k // 2
    half = ch // 2
    g = u.reshape(bsz, rows, GRID_W, ch)
    kh = w[:, :half][None, :, None, :]
    kv = w[:, half:][:, None, None, :]
    yh = lax.conv_general_dilated(g[..., :half], kh, (1, 1), [(0, 0), (pad, pad)],
                                  dimension_numbers=("NHWC", "HWIO", "NHWC"),
                                  feature_group_count=half)
    yv = lax.conv_general_dilated(g[..., half:], kv, (1, 1), [(pad, pad), (0, 0)],
                                  dimension_numbers=("NHWC", "HWIO", "NHWC"),
                                  feature_group_count=ch - half)
    return jnp.concatenate([yh, yv], axis=-1).reshape(bsz, seqlen, ch) + b


def ssd_chunked(xh, dt, a, bm, cm, h0):
    bsz, seqlen, nh, hd = xh.shape
    ng, ns = bm.shape[2], bm.shape[3]
    ne = nh // ng
    nc = seqlen // SSM_CHUNK
    dtype = xh.dtype
    xs = (xh * dt[..., None]).reshape(bsz, nc, SSM_CHUNK, ng, ne, hd)
    da = (dt.astype(jnp.float32) * a.astype(jnp.float32)).reshape(bsz, nc, SSM_CHUNK, ng, ne)
    cs = jnp.cumsum(da, axis=2)
    bc = bm.reshape(bsz, nc, SSM_CHUNK, ng, ns)
    cc = cm.reshape(bsz, nc, SSM_CHUNK, ng, ns)
    seg = cs[:, :, :, None] - cs[:, :, None, :]
    scan_order = jnp.tril(jnp.ones((SSM_CHUNK, SSM_CHUNK), dtype=bool))[None, None, :, :, None, None]
    decay = jnp.exp(jnp.where(scan_order, seg, -jnp.inf)).astype(dtype)
    scores = jnp.einsum("bclgn,bcsgn->bclsg", cc, bc)
    y_diag = jnp.einsum("bclsge,bcsgep->bclgep", scores[..., None] * decay, xs)
    w_state = jnp.exp(cs[:, :, -1:] - cs).astype(dtype)
    chunk_states = jnp.einsum("bclgn,bclge,bclgep->bcgepn", bc, w_state, xs)
    chunk_decay = jnp.exp(cs[:, :, -1]).astype(dtype)

    def step(h, inp):
        s, d = inp
        return h * d[..., None, None] + s, h

    _, h_prev = lax.scan(step, h0.reshape(bsz, ng, ne, hd, ns).astype(dtype),
                         (jnp.moveaxis(chunk_states, 1, 0), jnp.moveaxis(chunk_decay, 1, 0)))
    y_off = jnp.einsum("bclgn,cbgepn,bclge->bclgep", cc, h_prev, jnp.exp(cs).astype(dtype))
    return (y_diag + y_off).reshape(bsz, seqlen, nh, hd)


def ssd_final_state(xh, dt, a, bm):
    bsz, seqlen, nh, hd = xh.shape
    ng, ns = bm.shape[2], bm.shape[3]
    ne = nh // ng
    cs = jnp.cumsum(dt.astype(jnp.float32) * a.astype(jnp.float32), axis=1)
    w = (jnp.exp(cs[:, -1:] - cs) * dt.astype(jnp.float32)).astype(xh.dtype).reshape(bsz, seqlen, ng, ne)
    st = jnp.einsum("blgn,blge,blgep->bgepn", bm, w, xh.reshape(bsz, seqlen, ng, ne, hd))
    return st.reshape(bsz, nh, hd, ns)


def ctx_ssd_states(hc, w_in, conv_w, conv_b, dtb_f, dtb_b, alog_f, alog_b):
    bsz, clen, _ = hc.shape
    xb = jax.nn.silu(dwconv1d(hc @ w_in[:, OFF_X:OFF_C], conv_w[:, :D_SSM + GN], conv_b[:D_SSM + GN]))
    dt_raw = hc @ w_in[:, OFF_DT:OFF_GLU]
    xh = xb[..., :D_SSM].reshape(bsz, clen, SSM_HEADS, SSM_HEAD_DIM)
    bm = xb[..., D_SSM:].reshape(bsz, clen, SSM_GROUPS, SSM_STATE)
    dt_f = jax.nn.softplus(dt_raw[..., :SSM_HEADS] + dtb_f)
    dt_b = jax.nn.softplus(dt_raw[..., SSM_HEADS:] + dtb_b)
    s_f = ssd_final_state(xh, dt_f, -jnp.exp(alog_f), bm)
    s_b = ssd_final_state(_flip(xh), _flip(dt_b), -jnp.exp(alog_b), _flip(bm))
    return s_f, s_b


def mixer(h, w_in, w_out, conv_w, conv_b, dtb_f, dtb_b, alog_f, alog_b, d_skip, norm_w,
          cw, cb, ln_w, ln_b, h0_f, h0_b, rows):
    bsz, seqlen, _ = h.shape
    proj = h @ w_in
    z = proj[..., OFF_Z:OFF_X]
    xbc = jax.nn.silu(dwconv1d(proj[..., OFF_X:OFF_DT], conv_w, conv_b))
    dt_raw = proj[..., OFF_DT:OFF_GLU]
    glu = proj[..., OFF_GLU:]
    xh = xbc[..., :D_SSM].reshape(bsz, seqlen, SSM_HEADS, SSM_HEAD_DIM)
    bm = xbc[..., D_SSM:D_SSM + GN].reshape(bsz, seqlen, SSM_GROUPS, SSM_STATE)
    cm = xbc[..., D_SSM + GN:].reshape(bsz, seqlen, SSM_GROUPS, SSM_STATE)
    dt_f = jax.nn.softplus(dt_raw[..., :SSM_HEADS] + dtb_f)
    dt_b = jax.nn.softplus(dt_raw[..., SSM_HEADS:] + dtb_b)
    y_f = ssd_chunked(xh, dt_f, -jnp.exp(alog_f), bm, cm, h0_f)
    y_b = _flip(ssd_chunked(_flip(xh), _flip(dt_b), -jnp.exp(alog_b), _flip(bm), _flip(cm), h0_b))
    y = (y_f + y_b + d_skip[:, None] * xh).reshape(bsz, seqlen, D_SSM)
    y = group_rmsnorm(y * jax.nn.silu(z), norm_w, SSM_GROUPS)
    u = glu[..., :D_CONV] * jax.nn.sigmoid(glu[..., D_CONV:])
    if rows is None:
        u = dwconv1d(u, cw, cb)
    else:
        u = axial_dwconv(u, cw, cb, rows)
    u = jax.nn.silu(layernorm(u, ln_w, ln_b))
    return jnp.concatenate([y, u], axis=-1) @ w_out


def setup_inputs(seed: int = 0) -> dict:
    key = jax.random.key(seed)
    ks = iter(jax.random.split(key, 40))

    def nrm(shape, scale):
        return jax.random.normal(next(ks), shape, jnp.float32) * scale

    L = DEPTH
    d_in_scale = D_MODEL ** -0.5
    u_dt = jax.random.uniform(next(ks), (2, L, SSM_HEADS), jnp.float32)
    dt0 = jnp.exp(u_dt * (math.log(0.1) - math.log(1e-3)) + math.log(1e-3))
    dt_bias = dt0 + jnp.log(-jnp.expm1(-dt0))
    a_log = jnp.log(jax.random.uniform(next(ks), (2, L, SSM_HEADS), jnp.float32, 1.0, 16.0))
    return {
        "x": nrm((BATCH, SEQ, D_MODEL), 1.0),
        "c": nrm((BATCH, D_MODEL), 1.0),
        "ctx": nrm((BATCH, CTX_LEN, D_MODEL), 1.0),
        "c_ctx": nrm((D_MODEL,), 1.0),
        "w_mod": nrm((L, D_MODEL, N_MOD * D_MODEL), 0.5 * d_in_scale),
        "b_mod": nrm((L, N_MOD * D_MODEL), 0.02),
        "norm_ffn1": 1.0 + nrm((L, D_MODEL), 0.02),
        "ffn1_gate": nrm((L, D_MODEL, D_FF), d_in_scale),
        "ffn1_up": nrm((L, D_MODEL, D_FF), d_in_scale),
        "ffn1_down": nrm((L, D_FF, D_MODEL), D_FF ** -0.5),
        "norm_mix": 1.0 + nrm((L, D_MODEL), 0.02),
        "w_in": nrm((L, D_MODEL, D_IN_PROJ), d_in_scale),
        "ssm_conv_w": nrm((L, SSM_CONV_W, D_SSM + 2 * GN), SSM_CONV_W ** -0.5),
        "ssm_conv_b": nrm((L, D_SSM + 2 * GN), 0.02),
        "dt_bias_fwd": dt_bias[0],
        "dt_bias_bwd": dt_bias[1],
        "a_log_fwd": a_log[0],
        "a_log_bwd": a_log[1],
        "ssm_d": 1.0 + nrm((L, SSM_HEADS), 0.02),
        "ssm_norm_w": 1.0 + nrm((L, D_SSM), 0.02),
        "cconv_w": nrm((L, CONF_KERNEL, D_CONV), CONF_KERNEL ** -0.5),
        "cconv_b": nrm((L, D_CONV), 0.02),
        "cconv_ln_w": 1.0 + nrm((L, D_CONV), 0.02),
        "cconv_ln_b": nrm((L, D_CONV), 0.02),
        "w_out": nrm((L, D_INNER, D_MODEL), D_INNER ** -0.5),
        "norm_ffn2": 1.0 + nrm((L, D_MODEL), 0.02),
        "ffn2_gate": nrm((L, D_MODEL, D_FF), d_in_scale),
        "ffn2_up": nrm((L, D_MODEL, D_FF), d_in_scale),
        "ffn2_down": nrm((L, D_FF, D_MODEL), D_FF ** -0.5),
        "final_norm": 1.0 + nrm((D_MODEL,), 0.02),
    }


def reference(x, c, ctx, c_ctx, w_mod, b_mod, norm_ffn1, ffn1_gate, ffn1_up, ffn1_down, norm_mix,
              w_in, ssm_conv_w, ssm_conv_b, dt_bias_fwd, dt_bias_bwd, a_log_fwd, a_log_bwd, ssm_d,
              ssm_norm_w, cconv_w, cconv_b, cconv_ln_w, cconv_ln_b, w_out, norm_ffn2, ffn2_gate,
              ffn2_up, ffn2_down, final_norm):
    bsz = x.shape[0]
    rows = x.shape[1] // GRID_W
    xc = ctx
    for i in range(DEPTH):
        last = i == DEPTH - 1
        mod = (jax.nn.silu(c) @ w_mod[i] + b_mod[i]).reshape(bsz, N_MOD, 1, D_MODEL)
        n_c = 5 if last else N_MOD
        mod_c = (jax.nn.silu(c_ctx) @ w_mod[i][:, :n_c * D_MODEL] + b_mod[i][:n_c * D_MODEL]).reshape(n_c, 1, D_MODEL)
        x = x + 0.5 * mod[:, 2] * swiglu(modulate(rmsnorm(x, norm_ffn1[i]), mod[:, 0], mod[:, 1]),
                                         ffn1_gate[i], ffn1_up[i], ffn1_down[i])
        xc = xc + 0.5 * mod_c[2] * swiglu(modulate(rmsnorm(xc, norm_ffn1[i]), mod_c[0], mod_c[1]),
                                          ffn1_gate[i], ffn1_up[i], ffn1_down[i])
        hx = modulate(rmsnorm(x, norm_mix[i]), mod[:, 3], mod[:, 4])
        hc = modulate(rmsnorm(xc, norm_mix[i]), mod_c[3], mod_c[4])
        s_f, s_b = ctx_ssd_states(hc, w_in[i], ssm_conv_w[i], ssm_conv_b[i], dt_bias_fwd[i],
                                  dt_bias_bwd[i], a_log_fwd[i], a_log_bwd[i])
        x = x + mod[:, 5] * mixer(hx, w_in[i], w_out[i], ssm_conv_w[i], ssm_conv_b[i], dt_bias_fwd[i],
                                  dt_bias_bwd[i], a_log_fwd[i], a_log_bwd[i], ssm_d[i], ssm_norm_w[i],
                                  cconv_w[i], cconv_b[i], cconv_ln_w[i], cconv_ln_b[i], s_f, s_b, rows)
        if not last:
            zero_state = jnp.zeros_like(s_f)
            xc = xc + mod_c[5] * mixer(hc, w_in[i], w_out[i], ssm_conv_w[i], ssm_conv_b[i], dt_bias_fwd[i],
                                       dt_bias_bwd[i], a_log_fwd[i], a_log_bwd[i], ssm_d[i], ssm_norm_w[i],
                                       cconv_w[i], cconv_b[i], cconv_ln_w[i], cconv_ln_b[i],
                                       zero_state, zero_state, None)
        x = x + 0.5 * mod[:, 8] * swiglu(modulate(rmsnorm(x, norm_ffn2[i]), mod[:, 6], mod[:, 7]),
                                         ffn2_gate[i], ffn2_up[i], ffn2_down[i])
        if not last:
            xc = xc + 0.5 * mod_c[8] * swiglu(modulate(rmsnorm(xc, norm_ffn2[i]), mod_c[6], mod_c[7]),
                                              ffn2_gate[i], ffn2_up[i], ffn2_down[i])
    return rmsnorm(x, final_norm)
```

```python
import functools

import jax
import jax.numpy as jnp
from jax import lax
from jax.experimental import pallas as pl
from jax.experimental.pallas import tpu as pltpu

F32 = jnp.float32
BF16 = jnp.bfloat16

EPS = 1e-6
GRID_W = 64
HEAD_DIM = 64
N_GROUPS = 2
N_STATE = 128
CHUNK = 128
N_MOD = 9
LANES = 128
VMEM_LIMIT = 56 * 1024 * 1024


def _sigmoid(x):
    return 1.0 / (1.0 + jnp.exp(-x))


def _silu(x):
    return x * _sigmoid(x)


def _softplus(x):
    return jnp.maximum(x, 0.0) + jnp.log1p(jnp.exp(-jnp.abs(x)))


def _dot(a, b):
    return jnp.dot(a, b, preferred_element_type=F32)


def _split3(a):
    hi = a.astype(BF16)
    r1 = a - hi.astype(F32)
    mid = r1.astype(BF16)
    lo = (r1 - mid.astype(F32)).astype(BF16)
    return hi, mid, lo


def _dot_exact_rhs01(a, m01):
    hi, mid, lo = _split3(a)
    return _dot(hi, m01) + _dot(mid, m01) + _dot(lo, m01)


def _dot_exact_lhs01(m01, a):
    hi, mid, lo = _split3(a)
    return _dot(m01, hi) + _dot(m01, mid) + _dot(m01, lo)


def _modulated_norm(x, nw, shift, scale):
    ms = jnp.mean(x * x, axis=-1, keepdims=True)
    return (x * lax.rsqrt(ms + EPS) * nw) * (1.0 + scale) + shift


def _resident(shape):
    return pl.BlockSpec(shape, lambda *_: (0,) * len(shape), pipeline_mode=pl.Buffered(1))


def _params(sem):
    return pltpu.CompilerParams(dimension_semantics=sem, vmem_limit_bytes=VMEM_LIMIT)


def _mod_kernel(c_ref, w_ref, b_ref, o_ref):
    a = _silu(c_ref[...]).astype(BF16)
    o_ref[...] = _dot(a, w_ref[...].astype(BF16)) + b_ref[...]


def _mod_call(c_all, w, b):
    m, d = c_all.shape
    n = w.shape[1]
    tn = 1024
    return pl.pallas_call(
        _mod_kernel,
        grid=(n // tn,),
        in_specs=[pl.BlockSpec((m, d), lambda j: (0, 0)),
                  pl.BlockSpec((d, tn), lambda j: (0, j)),
                  pl.BlockSpec((1, tn), lambda j: (0, j))],
        out_specs=pl.BlockSpec((m, tn), lambda j: (0, j)),
        out_shape=jax.ShapeDtypeStruct((m, n), F32),
        compiler_params=_params(("arbitrary",)),
        name="mod_proj",
    )(c_all, w, b)


def _ffn_kernel(*refs, mod_idx, tf, has_final):
    if has_final:
        x_ref, mod_ref, nw_ref, wg_ref, wu_ref, wd_ref, fn_ref, o_ref, acc_ref = refs
    else:
        x_ref, mod_ref, nw_ref, wg_ref, wu_ref, wd_ref, o_ref, acc_ref = refs
    x = x_ref[...]
    shift = mod_ref[mod_idx:mod_idx + 1, :]
    scale = mod_ref[mod_idx + 1:mod_idx + 2, :]
    gate = mod_ref[mod_idx + 2:mod_idx + 3, :]
    h = _modulated_norm(x, nw_ref[...], shift, scale).astype(BF16)
    d_ff = wg_ref.shape[1]
    for j in range(d_ff // tf):
        g = _dot(h, wg_ref[:, j * tf:(j + 1) * tf])
        u = _dot(h, wu_ref[:, j * tf:(j + 1) * tf])
        a = (_silu(g) * u).astype(BF16)
        part = _dot(a, wd_ref[j * tf:(j + 1) * tf, :])
        if j == 0:
            acc_ref[...] = part
        else:
            acc_ref[...] += part
    out = x + (0.5 * gate) * acc_ref[...]
    if has_final:
        ms = jnp.mean(out * out, axis=-1, keepdims=True)
        out = out * lax.rsqrt(ms + EPS) * fn_ref[...]
    o_ref[...] = out


def _ffn_call(x, mod, rows_per_mod, mod_idx, nw, wg, wu, wd, final_w=None, name="ffn"):
    t, d = x.shape
    f = wg.shape[1]
    tm = _row_tile(t, rows_per_mod, 1024)
    tiles_per_mod = rows_per_mod // tm
    has_final = final_w is not None
    in_specs = [pl.BlockSpec((tm, d), lambda i: (i, 0)),
                pl.BlockSpec((None, N_MOD, d), lambda i: (i // tiles_per_mod, 0, 0)),
                _resident((1, d)), _resident((d, f)), _resident((d, f)), _resident((f, d))]
    args = [x, mod, nw, wg, wu, wd]
    if has_final:
        in_specs.append(_resident((1, d)))
        args.append(final_w)
    return pl.pallas_call(
        functools.partial(_ffn_kernel, mod_idx=mod_idx, tf=256, has_final=has_final),
        grid=(t // tm,),
        in_specs=in_specs,
        out_specs=pl.BlockSpec((tm, d), lambda i: (i, 0)),
        out_shape=jax.ShapeDtypeStruct((t, d), F32),
        scratch_shapes=[pltpu.VMEM((tm, d), F32)],
        compiler_params=_params(("parallel",)),
        name=name,
    )(*args)


def _row_tile(t, rows_per_mod, target):
    tm = min(target, rows_per_mod, t)
    while rows_per_mod % tm or t % tm:
        tm //= 2
    return tm


def _inproj_kernel(x_ref, mod_ref, nw_ref, w_ref, dtb_ref, zs_ref, xbc_ref, dt_ref, u_ref, *, d_ssm, d_xbc, d_conv):
    x = x_ref[...]
    h = _modulated_norm(x, nw_ref[...], mod_ref[3:4, :], mod_ref[4:5, :]).astype(BF16)
    tc = 512
    off = 0
    for j in range(d_ssm // tc):
        z = _dot(h, w_ref[:, off + j * tc:off + (j + 1) * tc])
        zs_ref[:, j * tc:(j + 1) * tc] = _silu(z).astype(BF16)
    off += d_ssm
    for j in range(d_xbc // tc):
        v = _dot(h, w_ref[:, off + j * tc:off + (j + 1) * tc])
        xbc_ref[:, j * tc:(j + 1) * tc] = v.astype(BF16)
    off += d_xbc
    dt_raw = _dot(h, w_ref[:, off:off + LANES])
    dt_ref[...] = _softplus(dt_raw + dtb_ref[...])
    off += LANES
    for j in range(d_conv // tc):
        ga = _dot(h, w_ref[:, off + j * tc:off + (j + 1) * tc])
        gb = _dot(h, w_ref[:, off + d_conv + j * tc:off + d_conv + (j + 1) * tc])
        u_ref[:, j * tc:(j + 1) * tc] = (ga * _sigmoid(gb)).astype(BF16)


def _inproj_call(x, mod, rows_per_mod, nw, w_cat, dtb, d_ssm, d_xbc, d_conv, name):
    t, d = x.shape
    ncat = w_cat.shape[1]
    tm = _row_tile(t, rows_per_mod, 1024)
    tiles_per_mod = rows_per_mod // tm
    row = lambda i: (i, 0)
    return pl.pallas_call(
        functools.partial(_inproj_kernel, d_ssm=d_ssm, d_xbc=d_xbc, d_conv=d_conv),
        grid=(t // tm,),
        in_specs=[pl.BlockSpec((tm, d), row),
                  pl.BlockSpec((None, N_MOD, d), lambda i: (i // tiles_per_mod, 0, 0)),
                  _resident((1, d)), _resident((d, ncat)), _resident((1, LANES))],
        out_specs=[pl.BlockSpec((tm, d_ssm), row), pl.BlockSpec((tm, d_xbc), row),
                   pl.BlockSpec((tm, LANES), row), pl.BlockSpec((tm, d_conv), row)],
        out_shape=[jax.ShapeDtypeStruct((t, d_ssm), BF16), jax.ShapeDtypeStruct((t, d_xbc), BF16),
                   jax.ShapeDtypeStruct((t, LANES), F32), jax.ShapeDtypeStruct((t, d_conv), BF16)],
        compiler_params=_params(("parallel",)),
        name=name,
    )(x, mod, nw, w_cat, dtb)


def _conv_silu_rows(src_ref, w_ref, b_ref, dst_ref, seqlen, out_dtype):
    taps = w_ref.shape[0]
    pad = taps // 2
    halo = 16
    blk = 128
    w = w_ref[...]
    b = b_ref[...]
    for i in range(seqlen // blk):
        r0 = i * blk
        cur = src_ref[r0:r0 + blk, :].astype(F32)
        zeros = jnp.zeros((halo, cur.shape[1]), F32)
        prev = src_ref[r0 - halo:r0, :].astype(F32) if i > 0 else zeros
        nxt = src_ref[r0 + blk:r0 + blk + halo, :].astype(F32) if r0 + blk < seqlen else zeros
        win = jnp.concatenate([prev, cur, nxt], axis=0)
        acc = jnp.zeros_like(cur)
        for k in range(taps):
            s = halo - pad + k
            acc = acc + win[s:s + blk, :] * w[k:k + 1, :]
        dst_ref[r0:r0 + blk, :] = _silu(acc + b).astype(out_dtype)


def _roll_heads_to_front(v, first_lane):
    return pltpu.roll(v, (LANES - first_lane) % LANES, axis=1)


def _ssd_kernel(xr_ref, br_ref, cr_ref, zs_ref, dt_ref, cwx_ref, cwb_ref, cwc_ref, cbx_ref, cbb_ref, cbc_ref,
                alog_ref, dsk_ref, nw_ref, h0_ref, e_ref, tl_ref, tu_ref, y_ref,
                xs_ref, bs_ref, cc_ref, dtd_ref, yacc_ref, st_ref, *, seqlen, heads_per_group, n_heads):
    g = pl.program_id(1)
    q = CHUNK
    nchunks = seqlen // q
    _conv_silu_rows(xr_ref, cwx_ref, cbx_ref, xs_ref, seqlen, F32)
    _conv_silu_rows(br_ref, cwb_ref, cbb_ref, bs_ref, seqlen, F32)
    _conv_silu_rows(cr_ref, cwc_ref, cbc_ref, cc_ref, seqlen, BF16)

    e01 = e_ref[...]
    rows = lax.broadcasted_iota(jnp.int32, (q, q), 0)
    cols = lax.broadcasted_iota(jnp.int32, (q, q), 1)
    lane_lo = lax.broadcasted_iota(jnp.int32, (q, LANES), 1) < HEAD_DIM
    dsk = dsk_ref[...]
    nw = nw_ref[...]

    for dirn in range(2):
        first_lane = dirn * n_heads + g * heads_per_group
        for i in range(seqlen // 256):
            dtd_ref[i * 256:(i + 1) * 256, :] = _roll_heads_to_front(dt_ref[i * 256:(i + 1) * 256, :], first_lane)
        a_row = -jnp.exp(_roll_heads_to_front(jnp.broadcast_to(alog_ref[...], (8, LANES)), first_lane))[0:1, :]
        st_ref[...] = h0_ref[dirn]
        t01 = tl_ref[...] if dirn == 0 else tu_ref[...]
        mask = (rows >= cols) if dirn == 0 else (rows <= cols)
        end = q - 1 if dirn == 0 else 0

        def chunk(ci, carry):
            c = ci if dirn == 0 else nchunks - 1 - ci
            r0 = pl.multiple_of(c * q, q)
            dt = dtd_ref[pl.ds(r0, q), :]
            cs = _dot_exact_lhs01(t01, dt * a_row)
            cs_t = cs.T
            x = xs_ref[pl.ds(r0, q), :]
            bc_t = bs_ref[pl.ds(r0, q), :].T.astype(BF16)
            cc = cc_ref[pl.ds(r0, q), :]
            cc32 = cc.astype(F32)
            gmat = _dot(cc, bc_t)
            xdt = x * _dot_exact_rhs01(dt, e01)
            xdt_b = xdt.astype(BF16)
            cs_end = cs[end:end + 1, :]
            w_state = jnp.exp(cs_end - cs)
            xw = (xdt * _dot_exact_rhs01(w_state, e01)).astype(BF16)
            dec = _dot_exact_rhs01(jnp.broadcast_to(jnp.exp(cs_end), (8, LANES)), e01)[0:1, :]
            ecs = jnp.exp(cs)
            st = st_ref[...]
            st_b = st.astype(BF16)
            ys = []
            for p in range(heads_per_group // 2):
                sl = slice(p * LANES, (p + 1) * LANES)
                rhs = jnp.concatenate([xdt_b[:, sl], st_b[:, sl]], axis=0)
                lhs = []
                for hh in range(2):
                    h = 2 * p + hh
                    seg = cs[:, h:h + 1] - cs_t[h:h + 1, :]
                    decay = jnp.exp(jnp.where(mask, seg, -jnp.inf))
                    m = (gmat * decay).astype(BF16)
                    ce = (cc32 * ecs[:, h:h + 1]).astype(BF16)
                    lhs.append(jnp.concatenate([m, ce], axis=1))
                res = _dot(jnp.concatenate(lhs, axis=0), rhs)
                ys.append(jnp.where(lane_lo, res[:q], res[q:]))
            y = jnp.concatenate(ys, axis=1)
            st_ref[...] = dec * st + _dot(bc_t, xw)
            if dirn == 0:
                yacc_ref[pl.ds(r0, q), :] = y
            else:
                y = y + yacc_ref[pl.ds(r0, q), :] + dsk * x
                y = y * zs_ref[pl.ds(r0, q), :].astype(F32)
                ms = jnp.mean(y * y, axis=-1, keepdims=True)
                y_ref[pl.ds(r0, q), :] = (y * lax.rsqrt(ms + EPS) * nw).astype(BF16)
            return carry

        lax.fori_loop(0, nchunks, chunk, 0)


def _ssd_call(xbc, zs, dt, conv_w, conv_b, alog, dsk_e, norm_w, h0, bsz, seqlen, d_ssm, name="ssd_scan"):
    n = N_STATE
    gw = d_ssm // N_GROUPS
    hpg = gw // HEAD_DIM
    n_heads = d_ssm // HEAD_DIM
    xb = d_ssm // n
    cb = xb + N_GROUPS
    e01 = (jnp.arange(LANES)[:, None] == (jnp.arange(gw)[None, :] // HEAD_DIM)).astype(BF16)
    tl = (jnp.arange(CHUNK)[:, None] >= jnp.arange(CHUNK)[None, :]).astype(BF16)
    tu = tl.T
    taps = conv_w.shape[0]
    return pl.pallas_call(
        functools.partial(_ssd_kernel, seqlen=seqlen, heads_per_group=hpg, n_heads=n_heads),
        grid=(bsz, N_GROUPS),
        in_specs=[pl.BlockSpec((seqlen, gw), lambda b, g: (b, g)),
                  pl.BlockSpec((seqlen, n), lambda b, g: (b, xb + g)),
                  pl.BlockSpec((seqlen, n), lambda b, g: (b, cb + g)),
                  pl.BlockSpec((seqlen, gw), lambda b, g: (b, g)),
                  pl.BlockSpec((seqlen, LANES), lambda b, g: (b, 0)),
                  pl.BlockSpec((taps, gw), lambda b, g: (0, g)),
                  pl.BlockSpec((taps, n), lambda b, g: (0, xb + g)),
                  pl.BlockSpec((taps, n), lambda b, g: (0, cb + g)),
                  pl.BlockSpec((1, gw), lambda b, g: (0, g)),
                  pl.BlockSpec((1, n), lambda b, g: (0, xb + g)),
                  pl.BlockSpec((1, n), lambda b, g: (0, cb + g)),
                  pl.BlockSpec((1, LANES), lambda b, g: (0, 0)),
                  pl.BlockSpec((1, gw), lambda b, g: (0, g)),
                  pl.BlockSpec((1, gw), lambda b, g: (0, g)),
                  pl.BlockSpec((None, 2, None, n, gw), lambda b, g: (b, 0, g, 0, 0)),
                  pl.BlockSpec((LANES, gw), lambda b, g: (0, 0)),
                  pl.BlockSpec((CHUNK, CHUNK), lambda b, g: (0, 0)),
                  pl.BlockSpec((CHUNK, CHUNK), lambda b, g: (0, 0))],
        out_specs=pl.BlockSpec((seqlen, gw), lambda b, g: (b, g)),
        out_shape=jax.ShapeDtypeStruct((bsz * seqlen, d_ssm), BF16),
        scratch_shapes=[pltpu.VMEM((seqlen, gw), F32), pltpu.VMEM((seqlen, n), F32),
                        pltpu.VMEM((seqlen, n), BF16), pltpu.VMEM((seqlen, LANES), F32),
                        pltpu.VMEM((seqlen, gw), F32), pltpu.VMEM((n, gw), F32)],
        compiler_params=_params(("parallel", "arbitrary")),
        name=name,
    )(xbc, xbc, xbc, zs, dt, conv_w, conv_w, conv_w, conv_b, conv_b, conv_b, alog, dsk_e, norm_w, h0, e01, tl, tu)


def _ctx_state_kernel(xr_ref, br_ref, dt_ref, cwx_ref, cwb_ref, cbx_ref, cbb_ref, alog_ref, e_ref, su_ref, sl_ref,
                      h0_ref, xs_ref, bs_ref, *, seqlen, heads_per_group, n_heads):
    g = pl.program_id(1)
    _conv_silu_rows(xr_ref, cwx_ref, cbx_ref, xs_ref, seqlen, F32)
    _conv_silu_rows(br_ref, cwb_ref, cbb_ref, bs_ref, seqlen, F32)
    e01 = e_ref[...]
    x = xs_ref[...]
    b_t = bs_ref[...].T.astype(BF16)
    for dirn in range(2):
        first_lane = dirn * n_heads + g * heads_per_group
        dt = _roll_heads_to_front(dt_ref[...], first_lane)
        a_row = -jnp.exp(_roll_heads_to_front(jnp.broadcast_to(alog_ref[...], (8, LANES)), first_lane))[0:1, :]
        s01 = su_ref[...] if dirn == 0 else sl_ref[...]
        rest = _dot_exact_lhs01(s01, dt * a_row)
        wgt = jnp.exp(rest) * dt
        xw = (x * _dot_exact_rhs01(wgt, e01)).astype(BF16)
        h0_ref[dirn] = _dot(b_t, xw)


def _ctx_state_call(xbc, dt, conv_w, conv_b, alog, bsz, seqlen, d_ssm, name="ctx_states"):
    n = N_STATE
    gw = d_ssm // N_GROUPS
    hpg = gw // HEAD_DIM
    n_heads = d_ssm // HEAD_DIM
    xb = d_ssm // n
    e01 = (jnp.arange(LANES)[:, None] == (jnp.arange(gw)[None, :] // HEAD_DIM)).astype(BF16)
    su = (jnp.arange(seqlen)[:, None] < jnp.arange(seqlen)[None, :]).astype(BF16)
    sl = su.T
    taps = conv_w.shape[0]
    return pl.pallas_call(
        functools.partial(_ctx_state_kernel, seqlen=seqlen, heads_per_group=hpg, n_heads=n_heads),
        grid=(bsz, N_GROUPS),
        in_specs=[pl.BlockSpec((seqlen, gw), lambda b, g: (b, g)),
                  pl.BlockSpec((seqlen, n), lambda b, g: (b, xb + g)),
                  pl.BlockSpec((seqlen, LANES), lambda b, g: (b, 0)),
                  pl.BlockSpec((taps, gw), lambda b, g: (0, g)),
                  pl.BlockSpec((taps, n), lambda b, g: (0, xb + g)),
                  pl.BlockSpec((1, gw), lambda b, g: (0, g)),
                  pl.BlockSpec((1, n), lambda b, g: (0, xb + g)),
                  pl.BlockSpec((1, LANES), lambda b, g: (0, 0)),
                  pl.BlockSpec((LANES, gw), lambda b, g: (0, 0)),
                  pl.BlockSpec((seqlen, seqlen), lambda b, g: (0, 0)),
                  pl.BlockSpec((seqlen, seqlen), lambda b, g: (0, 0))],
        out_specs=pl.BlockSpec((None, 2, None, n, gw), lambda b, g: (b, 0, g, 0, 0)),
        out_shape=jax.ShapeDtypeStruct((bsz, 2, N_GROUPS, n, gw), F32),
        scratch_shapes=[pltpu.VMEM((seqlen, gw), F32), pltpu.VMEM((seqlen, n), F32)],
        compiler_params=_params(("parallel", "arbitrary")),
        name=name,
    )(xbc, xbc, dt, conv_w, conv_w, conv_b, conv_b, alog, e01, su, sl)


def _cconv_kernel(u_ref, cw_ref, cb_ref, lw_ref, lb_ref, o_ref, colpad_ref, rowpad_ref, tmp_ref, *, rows):
    taps = cw_ref.shape[0]
    pad = taps // 2
    half = colpad_ref.shape[2]
    side = 16
    wcol = GRID_W

    colpad_ref[:, 0:side, :] = jnp.zeros((rows, side, half), F32)
    colpad_ref[:, side + wcol:, :] = jnp.zeros((rows, side, half), F32)
    rowpad_ref[0:pad] = jnp.zeros((pad, wcol, half), F32)
    rowpad_ref[pad + rows:] = jnp.zeros((pad, wcol, half), F32)

    def fill(r, carry):
        t0 = pl.multiple_of(r * wcol, wcol)
        colpad_ref[r, side:side + wcol, :] = u_ref[pl.ds(t0, wcol), 0:half].astype(F32)
        rowpad_ref[pad + r] = u_ref[pl.ds(t0, wcol), half:].astype(F32)
        return carry

    lax.fori_loop(0, rows, fill, 0)

    def row(r, carry):
        for j in range(half // LANES):
            sl = slice(j * LANES, (j + 1) * LANES)
            acc = jnp.zeros((wcol, LANES), F32)
            for k in range(taps):
                s = side - pad + k
                acc = acc + colpad_ref[r, s:s + wcol, sl] * cw_ref[k:k + 1, sl]
            tmp_ref[:, sl] = acc
            sl2 = slice(half + j * LANES, half + (j + 1) * LANES)
            acc = jnp.zeros((wcol, LANES), F32)
            for k in range(taps):
                acc = acc + rowpad_ref[r + k, :, sl] * cw_ref[k:k + 1, sl2]
            tmp_ref[:, sl2] = acc
        v = tmp_ref[...] + cb_ref[...]
        mu = jnp.mean(v, axis=-1, keepdims=True)
        vc = v - mu
        var = jnp.mean(vc * vc, axis=-1, keepdims=True)
        yv = vc * lax.rsqrt(var + EPS) * lw_ref[...] + lb_ref[...]
        t0 = pl.multiple_of(r * wcol, wcol)
        o_ref[pl.ds(t0, wcol), :] = _silu(yv).astype(BF16)
        return carry

    lax.fori_loop(0, rows, row, 0)


def _cconv_call(u, cw, cb, lw, lb, bsz, seqlen, name="axial_conv"):
    d = u.shape[1]
    rows = seqlen // GRID_W
    taps = cw.shape[0]
    pad = taps // 2
    half = d // 2
    return pl.pallas_call(
        functools.partial(_cconv_kernel, rows=rows),
        grid=(bsz,),
        in_specs=[pl.BlockSpec((seqlen, d), lambda b: (b, 0)),
                  _resident((taps, d)), _resident((1, d)), _resident((1, d)), _resident((1, d))],
        out_specs=pl.BlockSpec((seqlen, d), lambda b: (b, 0)),
        out_shape=jax.ShapeDtypeStruct(u.shape, BF16),
        scratch_shapes=[pltpu.VMEM((rows, GRID_W + 32, half), F32),
                        pltpu.VMEM((rows + 2 * pad, GRID_W, half), F32),
                        pltpu.VMEM((GRID_W, d), F32)],
        compiler_params=_params(("parallel",)),
        name=name,
    )(u, cw, cb, lw, lb)


def _outproj_kernel(x_ref, y_ref, u_ref, mod_ref, wy_ref, wu_ref, o_ref):
    mix = _dot(y_ref[...], wy_ref[...]) + _dot(u_ref[...], wu_ref[...])
    o_ref[...] = x_ref[...] + mod_ref[5:6, :] * mix


def _outproj_call(x, y, u, mod, rows_per_mod, wy, wu, name="out_proj"):
    t, d = x.shape
    tm = _row_tile(t, rows_per_mod, 1024)
    tiles_per_mod = rows_per_mod // tm
    row = lambda i: (i, 0)
    return pl.pallas_call(
        _outproj_kernel,
        grid=(t // tm,),
        in_specs=[pl.BlockSpec((tm, d), row), pl.BlockSpec((tm, y.shape[1]), row), pl.BlockSpec((tm, u.shape[1]), row),
                  pl.BlockSpec((None, N_MOD, d), lambda i: (i // tiles_per_mod, 0, 0)),
                  _resident(wy.shape), _resident(wu.shape)],
        out_specs=pl.BlockSpec((tm, d), row),
        out_shape=jax.ShapeDtypeStruct((t, d), F32),
        compiler_params=_params(("parallel",)),
        name=name,
    )(x, y, u, mod, wy, wu)


def kernel(x, c, ctx, c_ctx, w_mod, b_mod, norm_ffn1, ffn1_gate, ffn1_up, ffn1_down, norm_mix, w_in, ssm_conv_w, ssm_conv_b, dt_bias_fwd, dt_bias_bwd, a_log_fwd, a_log_bwd, ssm_d, ssm_norm_w, cconv_w, cconv_b, cconv_ln_w, cconv_ln_b, w_out, norm_ffn2, ffn2_gate, ffn2_up, ffn2_down, final_norm):
    bsz, seqlen, d = x.shape
    clen = ctx.shape[1]
    depth = w_mod.shape[0]
    d_ssm = ssm_norm_w.shape[1]
    d_conv = cconv_w.shape[2]
    d_xbc = ssm_conv_w.shape[2]
    n_heads = d_ssm // HEAD_DIM
    off_x = d_ssm
    off_dt = off_x + d_xbc
    off_glu = off_dt + 2 * n_heads
    assert seqlen % CHUNK == 0 and seqlen % GRID_W == 0 and 2 * n_heads <= LANES

    def pad_lanes(v):
        return jnp.pad(v, (0, LANES - v.shape[0])).reshape(1, LANES)

    xt = x.reshape(bsz * seqlen, d)
    xc = ctx.reshape(bsz * clen, d)
    assert depth == 1, "the context stream of non-final layers is not implemented"
    c_all = jnp.concatenate([c, c_ctx[None, :], jnp.zeros((-(bsz + 1) % 8, d), F32)], axis=0)
    for i in range(depth):
        last = i == depth - 1
        mod_all = _mod_call(c_all, w_mod[i], b_mod[i].reshape(1, -1))
        mod = mod_all[:bsz].reshape(bsz, N_MOD, d)
        mod_c = mod_all[bsz:bsz + 1].reshape(1, N_MOD, d)

        wg1, wu1, wd1 = ffn1_gate[i].astype(BF16), ffn1_up[i].astype(BF16), ffn1_down[i].astype(BF16)
        nw1 = norm_ffn1[i].reshape(1, d)
        xt = _ffn_call(xt, mod, seqlen, 0, nw1, wg1, wu1, wd1, name="ffn1")
        xc = _ffn_call(xc, mod_c, bsz * clen, 0, nw1, wg1, wu1, wd1, name="ffn1_ctx")

        wi = w_in[i]
        w_cat = jnp.concatenate(
            [wi[:, :off_dt], jnp.pad(wi[:, off_dt:off_glu], ((0, 0), (0, LANES - 2 * n_heads))), wi[:, off_glu:]],
            axis=1).astype(BF16)
        dtb = pad_lanes(jnp.concatenate([dt_bias_fwd[i], dt_bias_bwd[i]]))
        alog = pad_lanes(jnp.concatenate([a_log_fwd[i], a_log_bwd[i]]))
        nwm = norm_mix[i].reshape(1, d)
        conv_b = ssm_conv_b[i].reshape(1, -1)

        zs, xbc, dt, u = _inproj_call(xt, mod, seqlen, nwm, w_cat, dtb, d_ssm, d_xbc, d_conv, name="in_proj")
        _, xbc_c, dt_c, _ = _inproj_call(xc, mod_c, bsz * clen, nwm, w_cat, dtb, d_ssm, d_xbc, d_conv, name="in_proj_ctx")
        h0 = _ctx_state_call(xbc_c, dt_c, ssm_conv_w[i], conv_b, alog, bsz, clen, d_ssm)

        dsk_e = jnp.repeat(ssm_d[i], HEAD_DIM).reshape(1, d_ssm)
        y = _ssd_call(xbc, zs, dt, ssm_conv_w[i], conv_b, alog, dsk_e, ssm_norm_w[i].reshape(1, d_ssm), h0,
                      bsz, seqlen, d_ssm)
        uc = _cconv_call(u, cconv_w[i], cconv_b[i].reshape(1, -1), cconv_ln_w[i].reshape(1, -1),
                         cconv_ln_b[i].reshape(1, -1), bsz, seqlen)
        wo = w_out[i].astype(BF16)
        xt = _outproj_call(xt, y, uc, mod, seqlen, wo[:d_ssm], wo[d_ssm:])
        xt = _ffn_call(xt, mod, seqlen, 6, norm_ffn2[i].reshape(1, d), ffn2_gate[i].astype(BF16),
                       ffn2_up[i].astype(BF16), ffn2_down[i].astype(BF16),
                       final_w=final_norm.reshape(1, d) if last else None, name="ffn2")
    return xt.reshape(bsz, seqlen, d)
```

```python
import functools

import jax
import jax.numpy as jnp
from jax import lax
from jax.experimental import pallas as pl
from jax.experimental.pallas import tpu as pltpu

F32 = jnp.float32
BF16 = jnp.bfloat16

EPS = 1e-6
GRID_W = 64
HEAD_DIM = 64
N_GROUPS = 2
N_STATE = 128
CHUNK = 128
N_MOD = 9
LANES = 128
BF16_ROWS = 16
HALO = BF16_ROWS
CONV_STEP = 4
VMEM_LIMIT = 56 * 1024 * 1024


def _sigmoid(x):
    return 1.0 / (1.0 + jnp.exp(-x))


def _silu(x):
    return x * _sigmoid(x)


def _softplus(x):
    return jnp.maximum(x, 0.0) + jnp.log1p(jnp.exp(-jnp.abs(x)))


def _dot(a, b):
    return jnp.dot(a, b, preferred_element_type=F32)


def _split3(a):
    hi = a.astype(BF16)
    r1 = a - hi.astype(F32)
    mid = r1.astype(BF16)
    lo = (r1 - mid.astype(F32)).astype(BF16)
    return hi, mid, lo


def _dot_exact_rhs01(a, m01):
    hi, mid, lo = _split3(a)
    return _dot(hi, m01) + _dot(mid, m01) + _dot(lo, m01)


def _dot_exact_lhs01(m01, a):
    hi, mid, lo = _split3(a)
    return _dot(m01, hi) + _dot(m01, mid) + _dot(m01, lo)


def _modulated_norm(x, nw, shift, scale):
    ms = jnp.mean(x * x, axis=-1, keepdims=True)
    return (x * lax.rsqrt(ms + EPS) * nw) * (1.0 + scale) + shift


def _resident(shape):
    return pl.BlockSpec(shape, lambda *_: (0,) * len(shape), pipeline_mode=pl.Buffered(1))


def _params(sem):
    return pltpu.CompilerParams(dimension_semantics=sem, vmem_limit_bytes=VMEM_LIMIT)


def _mod_kernel(c_ref, w_ref, b_ref, o_ref):
    a = _silu(c_ref[...]).astype(BF16)
    o_ref[...] = _dot(a, w_ref[...].astype(BF16)) + b_ref[...]


def _mod_call(c_all, w, b):
    m, d = c_all.shape
    n = w.shape[1]
    tn = 1024
    return pl.pallas_call(
        _mod_kernel,
        grid=(n // tn,),
        in_specs=[pl.BlockSpec((m, d), lambda j: (0, 0)),
                  pl.BlockSpec((d, tn), lambda j: (0, j)),
                  pl.BlockSpec((1, tn), lambda j: (0, j))],
        out_specs=pl.BlockSpec((m, tn), lambda j: (0, j)),
        out_shape=jax.ShapeDtypeStruct((m, n), F32),
        compiler_params=_params(("arbitrary",)),
        name="mod_proj",
    )(c_all, w, b)


def _ffn_kernel(*refs, mod_idx, tf, has_final):
    if has_final:
        x_ref, mod_ref, nw_ref, wg_ref, wu_ref, wd_ref, fn_ref, o_ref, acc_ref = refs
    else:
        x_ref, mod_ref, nw_ref, wg_ref, wu_ref, wd_ref, o_ref, acc_ref = refs
    x = x_ref[...]
    shift = mod_ref[mod_idx:mod_idx + 1, :]
    scale = mod_ref[mod_idx + 1:mod_idx + 2, :]
    gate = mod_ref[mod_idx + 2:mod_idx + 3, :]
    h = _modulated_norm(x, nw_ref[...], shift, scale).astype(BF16)
    d_ff = wg_ref.shape[1]
    for j in range(d_ff // tf):
        g = _dot(h, wg_ref[:, j * tf:(j + 1) * tf])
        u = _dot(h, wu_ref[:, j * tf:(j + 1) * tf])
        a = (_silu(g) * u).astype(BF16)
        part = _dot(a, wd_ref[j * tf:(j + 1) * tf, :])
        if j == 0:
            acc_ref[...] = part
        else:
            acc_ref[...] += part
    out = x + (0.5 * gate) * acc_ref[...]
    if has_final:
        ms = jnp.mean(out * out, axis=-1, keepdims=True)
        out = out * lax.rsqrt(ms + EPS) * fn_ref[...]
    o_ref[...] = out


def _ffn_call(x, mod, rows_per_mod, mod_idx, nw, wg, wu, wd, final_w=None, name="ffn"):
    t, d = x.shape
    f = wg.shape[1]
    tm = _row_tile(t, rows_per_mod, 1024)
    tiles_per_mod = rows_per_mod // tm
    has_final = final_w is not None
    in_specs = [pl.BlockSpec((tm, d), lambda i: (i, 0)),
                pl.BlockSpec((None, N_MOD, d), lambda i: (i // tiles_per_mod, 0, 0)),
                _resident((1, d)), _resident((d, f)), _resident((d, f)), _resident((f, d))]
    args = [x, mod, nw, wg, wu, wd]
    if has_final:
        in_specs.append(_resident((1, d)))
        args.append(final_w)
    return pl.pallas_call(
        functools.partial(_ffn_kernel, mod_idx=mod_idx, tf=256, has_final=has_final),
        grid=(t // tm,),
        in_specs=in_specs,
        out_specs=pl.BlockSpec((tm, d), lambda i: (i, 0)),
        out_shape=jax.ShapeDtypeStruct((t, d), F32),
        scratch_shapes=[pltpu.VMEM((tm, d), F32)],
        compiler_params=_params(("parallel",)),
        name=name,
    )(*args)


def _row_tile(t, rows_per_mod, target):
    tm = min(target, rows_per_mod, t)
    while rows_per_mod % tm or t % tm:
        tm //= 2
    return tm


def _inproj_kernel(x_ref, xp_ref, xn_ref, mod_ref, nw_ref, w_ref, dtb_ref, cw_ref, cb_ref,
                   zs_ref, xbc_ref, dt_ref, u_ref, pre_ref, *, d_ssm, d_xbc, d_conv, tiles_per_seq):
    i = pl.program_id(0)
    tm = x_ref.shape[0]
    nw, shift, scale = nw_ref[...], mod_ref[3:4, :], mod_ref[4:5, :]
    h = _modulated_norm(x_ref[...], nw, shift, scale).astype(BF16)
    seq_pos = i % tiles_per_seq
    hp = jnp.where(seq_pos == 0, 0.0, _modulated_norm(xp_ref[...], nw, shift, scale)).astype(BF16)
    hn = jnp.where(seq_pos == tiles_per_seq - 1, 0.0, _modulated_norm(xn_ref[...], nw, shift, scale)).astype(BF16)
    h_ext = jnp.concatenate([hp, h, hn], axis=0)
    tc = pre_ref.shape[2]
    off_z, off_xbc, off_dt = 0, d_ssm, d_ssm + d_xbc
    off_glu = off_dt + LANES
    taps = cw_ref.shape[0]
    first = HALO - taps // 2
    blk = 128
    n_xbc = d_xbc // tc

    def project_xbc(j):
        pre_ref[j] = _dot(h_ext, w_ref[:, off_xbc + j * tc:off_xbc + (j + 1) * tc])

    def conv_xbc(j):
        for lb in range(tc // LANES):
            cols = slice(j * tc + lb * LANES, j * tc + (lb + 1) * LANES)
            lanes = slice(lb * LANES, (lb + 1) * LANES)
            for rb in range(tm // blk):
                acc = pre_ref[j, first + rb * blk:first + (rb + 1) * blk, lanes] * cw_ref[0:1, cols]
                for k in range(1, taps):
                    s = first + k + rb * blk
                    acc = acc + pre_ref[j, s:s + blk, lanes] * cw_ref[k:k + 1, cols]
                xbc_ref[rb * blk:(rb + 1) * blk, cols] = _silu(acc + cb_ref[:, cols]).astype(BF16)

    def project_z(j):
        z = _dot(h, w_ref[:, off_z + j * tc:off_z + (j + 1) * tc])
        zs_ref[:, j * tc:(j + 1) * tc] = _silu(z).astype(BF16)

    def project_glu(j):
        ga = _dot(h, w_ref[:, off_glu + j * tc:off_glu + (j + 1) * tc])
        gb = _dot(h, w_ref[:, off_glu + d_conv + j * tc:off_glu + d_conv + (j + 1) * tc])
        u_ref[:, j * tc:(j + 1) * tc] = (ga * _sigmoid(gb)).astype(BF16)

    others = [functools.partial(project_z, j) for j in range(d_ssm // tc)]
    others += [functools.partial(project_glu, j) for j in range(d_conv // tc)]
    project_xbc(0)
    for j in range(n_xbc):
        if j + 1 < n_xbc:
            project_xbc(j + 1)
        conv_xbc(j)
        for fn in others[j::n_xbc]:
            fn()
    dt_raw = _dot(h, w_ref[:, off_dt:off_dt + LANES])
    dt_ref[...] = _softplus(dt_raw + dtb_ref[...])


def _inproj_call(x, mod, rows_per_mod, seqlen, nw, w_cat, dtb, conv_w, conv_b, d_ssm, d_xbc, d_conv, name):
    t, d = x.shape
    ncat = w_cat.shape[1]
    taps = conv_w.shape[0]
    tm = _row_tile(t, min(rows_per_mod, seqlen), 1024)
    assert seqlen % tm == 0 and tm % HALO == 0 and taps // 2 <= HALO
    tiles_per_mod = rows_per_mod // tm
    halo_blocks = tm // HALO
    row = lambda i: (i, 0)
    return pl.pallas_call(
        functools.partial(_inproj_kernel, d_ssm=d_ssm, d_xbc=d_xbc, d_conv=d_conv, tiles_per_seq=seqlen // tm),
        grid=(t // tm,),
        in_specs=[pl.BlockSpec((tm, d), row),
                  pl.BlockSpec((HALO, d), lambda i: (jnp.maximum(i * halo_blocks - 1, 0), 0)),
                  pl.BlockSpec((HALO, d), lambda i: (jnp.minimum((i + 1) * halo_blocks, t // HALO - 1), 0)),
                  pl.BlockSpec((None, N_MOD, d), lambda i: (i // tiles_per_mod, 0, 0)),
                  _resident((1, d)), _resident((d, ncat)), _resident((1, LANES)),
                  _resident((taps, d_xbc)), _resident((1, d_xbc))],
        out_specs=[pl.BlockSpec((tm, d_ssm), row), pl.BlockSpec((tm, d_xbc), row),
                   pl.BlockSpec((tm, LANES), row), pl.BlockSpec((tm, d_conv), row)],
        out_shape=[jax.ShapeDtypeStruct((t, d_ssm), BF16), jax.ShapeDtypeStruct((t, d_xbc), BF16),
                   jax.ShapeDtypeStruct((t, LANES), F32), jax.ShapeDtypeStruct((t, d_conv), BF16)],
        scratch_shapes=[pltpu.VMEM((d_xbc // 512, tm + 2 * HALO, 512), F32)],
        compiler_params=_params(("parallel",)),
        name=name,
    )(x, x, x, mod, nw, w_cat, dtb, conv_w, conv_b)


def _roll_heads_to_front(v, first_lane):
    return pltpu.roll(v, (LANES - first_lane) % LANES, axis=1)


def _head_expansion(first_lane, width):
    lane = lax.broadcasted_iota(jnp.int32, (LANES, width), 0)
    head = lax.broadcasted_iota(jnp.int32, (LANES, width), 1) // HEAD_DIM
    return jnp.where(lane == head + first_lane, 1.0, 0.0).astype(BF16)


def _dot_rhs01_2term(a, m01):
    hi = a.astype(BF16)
    lo = (a - hi.astype(F32)).astype(BF16)
    return _dot(hi, m01) + _dot(lo, m01)


def _ssd_kernel(xs_ref, b_ref, c_ref, zs_ref, dt_ref, alog_ref, dsk_ref, nw_ref, h0_ref, tl_ref, y_ref,
                cs_ref, vt_ref, vtall_ref, xw_ref, dec_ref, g_ref, bt_ref, yacc_ref, st_ref,
                *, seqlen, heads_per_group, n_heads):
    g = pl.program_id(1)
    q = CHUNK
    nchunks = seqlen // q
    gw = xs_ref.shape[1]
    t01 = tl_ref[...]
    rows = lax.broadcasted_iota(jnp.int32, (q, q), 0)
    cols = lax.broadcasted_iota(jnp.int32, (q, q), 1)
    lane = lax.broadcasted_iota(jnp.int32, (q, LANES), 1)
    lane_lo = lane < HEAD_DIM
    bwd_lane = lane >= n_heads
    a_all = -jnp.exp(alog_ref[...])
    first_lanes = [dirn * n_heads + g * heads_per_group for dirn in range(2)]
    e_dir = [_head_expansion(fl, gw) for fl in first_lanes]
    dsk = dsk_ref[...]
    nw = nw_ref[...]

    def prepare(c, carry):
        r0 = pl.multiple_of(c * q, q)
        dt = dt_ref[pl.ds(r0, q), :]
        da = dt * a_all
        pre = _dot_exact_lhs01(t01, da)
        tot = pre[q - 1:q, :]
        cs_all = jnp.where(bwd_lane, tot - pre + da, pre)
        k_all = dt * jnp.exp(tot - cs_all)
        vtall = vtall_ref.at[c % 2]
        vtall[...] = (cs_all - jnp.log(dt)).T
        dec_all = jnp.broadcast_to(jnp.exp(tot), (8, LANES))
        bt = b_ref[pl.ds(r0, q), :].astype(F32).T.astype(BF16)
        bt_ref[c] = bt
        g_ref[c] = _dot(c_ref[pl.ds(r0, q), :], bt)
        x = xs_ref[pl.ds(r0, q), :].astype(F32)
        for dirn in range(2):
            fl = first_lanes[dirn]
            cs_ref[dirn, pl.ds(r0, q), :] = _roll_heads_to_front(cs_all, fl)
            vt_ref[dirn, c] = vtall[pl.ds(pl.multiple_of(fl, 8), 8), :]
            xw_ref[dirn, pl.ds(r0, q), :] = (x * _dot_rhs01_2term(k_all, e_dir[dirn])).astype(BF16)
            dec_ref[dirn, c] = _dot_exact_rhs01(dec_all, e_dir[dirn])
        return carry

    lax.fori_loop(0, nchunks, prepare, 0, unroll=2)

    def scan_chunk(c, dirn):
        r0 = pl.multiple_of(c * q, q)
        cs = cs_ref[dirn, pl.ds(r0, q), :]
        vt = vt_ref[dirn, c]
        gmat = g_ref[c]
        cc32 = c_ref[pl.ds(r0, q), :].astype(F32)
        x_b = xs_ref[pl.ds(r0, q), :]
        st = st_ref[dirn]
        st_b = st.astype(BF16)
        mask = (rows >= cols) if dirn == 0 else (rows <= cols)
        ys = []
        for p in range(heads_per_group // 2):
            sl = slice(p * LANES, (p + 1) * LANES)
            rhs = jnp.concatenate([x_b[:, sl], st_b[:, sl]], axis=0)
            lhs = []
            for hh in range(2):
                h = 2 * p + hh
                cs_col = jnp.broadcast_to(cs[:, h:h + 1], (q, LANES))
                decay = jnp.exp(jnp.where(mask, cs_col - vt[h:h + 1, :], -jnp.inf))
                m = (gmat * decay).astype(BF16)
                ce = (cc32 * jnp.exp(cs_col)).astype(BF16)
                lhs.append(jnp.concatenate([m, ce], axis=1))
            res = _dot(jnp.concatenate(lhs, axis=0), rhs)
            ys.append(jnp.where(lane_lo, res[:q], res[q:]))
        st_ref[dirn] = dec_ref[dirn, c][0:1, :] * st + _dot(bt_ref[c], xw_ref[dirn, pl.ds(r0, q), :])
        return jnp.concatenate(ys, axis=1)

    def finish(c, y):
        r0 = pl.multiple_of(c * q, q)
        y = y + yacc_ref[pl.ds(r0, q), :] + dsk * xs_ref[pl.ds(r0, q), :].astype(F32)
        y = y * zs_ref[pl.ds(r0, q), :].astype(F32)
        ms = jnp.mean(y * y, axis=-1, keepdims=True)
        y_ref[pl.ds(r0, q), :] = (y * lax.rsqrt(ms + EPS) * nw).astype(BF16)

    st_ref[...] = h0_ref[...]
    half = nchunks // 2

    def first_half(i, carry):
        cf, cb = i, nchunks - 1 - i
        yacc_ref[pl.ds(pl.multiple_of(cf * q, q), q), :] = scan_chunk(cf, 0)
        yacc_ref[pl.ds(pl.multiple_of(cb * q, q), q), :] = scan_chunk(cb, 1)
        return carry

    def second_half(i, carry):
        cf, cb = i, nchunks - 1 - i
        finish(cf, scan_chunk(cf, 0))
        finish(cb, scan_chunk(cb, 1))
        return carry

    lax.fori_loop(0, half, first_half, 0)
    lax.fori_loop(half, nchunks, second_half, 0)


def _ssd_call(xbc, zs, dt, alog, dsk_e, norm_w, h0, bsz, seqlen, d_ssm, name="ssd_scan"):
    n = N_STATE
    q = CHUNK
    gw = d_ssm // N_GROUPS
    hpg = gw // HEAD_DIM
    n_heads = d_ssm // HEAD_DIM
    xb = d_ssm // n
    cb = xb + N_GROUPS
    nchunks = seqlen // q
    assert nchunks % 2 == 0 and hpg % 8 == 0
    tl = (jnp.arange(q)[:, None] >= jnp.arange(q)[None, :]).astype(BF16)
    return pl.pallas_call(
        functools.partial(_ssd_kernel, seqlen=seqlen, heads_per_group=hpg, n_heads=n_heads),
        grid=(bsz, N_GROUPS),
        in_specs=[pl.BlockSpec((seqlen, gw), lambda b, g: (b, g)),
                  pl.BlockSpec((seqlen, n), lambda b, g: (b, xb + g)),
                  pl.BlockSpec((seqlen, n), lambda b, g: (b, cb + g)),
                  pl.BlockSpec((seqlen, gw), lambda b, g: (b, g)),
                  pl.BlockSpec((seqlen, LANES), lambda b, g: (b, 0)),
                  pl.BlockSpec((1, LANES), lambda b, g: (0, 0)),
                  pl.BlockSpec((1, gw), lambda b, g: (0, g)),
                  pl.BlockSpec((1, gw), lambda b, g: (0, g)),
                  pl.BlockSpec((None, 2, None, n, gw), lambda b, g: (b, 0, g, 0, 0)),
                  pl.BlockSpec((q, q), lambda b, g: (0, 0))],
        out_specs=pl.BlockSpec((seqlen, gw), lambda b, g: (b, g)),
        out_shape=jax.ShapeDtypeStruct((bsz * seqlen, d_ssm), BF16),
        scratch_shapes=[pltpu.VMEM((2, seqlen, LANES), F32),
                        pltpu.VMEM((2, nchunks, 8, q), F32),
                        pltpu.VMEM((2, LANES, q), F32),
                        pltpu.VMEM((2, seqlen, gw), BF16),
                        pltpu.VMEM((2, nchunks, 8, gw), F32),
                        pltpu.VMEM((nchunks, q, q), F32),
                        pltpu.VMEM((nchunks, n, q), BF16),
                        pltpu.VMEM((seqlen, gw), F32),
                        pltpu.VMEM((2, n, gw), F32)],
        compiler_params=_params(("parallel", "arbitrary")),
        name=name,
    )(xbc, xbc, xbc, zs, dt, alog, dsk_e, norm_w, h0, tl)


def _ctx_state_kernel(xs_ref, b_ref, dt_ref, alog_ref, su_ref, h0_ref, *, heads_per_group, n_heads):
    g = pl.program_id(1)
    seqlen, gw = xs_ref.shape
    x = xs_ref[...].astype(F32)
    b_t = b_ref[...].astype(F32).T.astype(BF16)
    dt = dt_ref[...]
    da = dt * -jnp.exp(alog_ref[...])
    later = _dot_exact_lhs01(su_ref[...], da)
    total = later[0:1, :] + da[0:1, :]
    bwd_lane = lax.broadcasted_iota(jnp.int32, (seqlen, LANES), 1) >= n_heads
    rest = jnp.where(bwd_lane, total - later - da, later)
    wgt = jnp.exp(rest) * dt
    for dirn in range(2):
        e01 = _head_expansion(dirn * n_heads + g * heads_per_group, gw)
        xw = (x * _dot_rhs01_2term(wgt, e01)).astype(BF16)
        h0_ref[dirn] = _dot(b_t, xw)


def _ctx_state_call(xbc, dt, alog, bsz, seqlen, d_ssm, name="ctx_states"):
    n = N_STATE
    gw = d_ssm // N_GROUPS
    hpg = gw // HEAD_DIM
    n_heads = d_ssm // HEAD_DIM
    xb = d_ssm // n
    su = (jnp.arange(seqlen)[:, None] < jnp.arange(seqlen)[None, :]).astype(BF16)
    return pl.pallas_call(
        functools.partial(_ctx_state_kernel, heads_per_group=hpg, n_heads=n_heads),
        grid=(bsz, N_GROUPS),
        in_specs=[pl.BlockSpec((seqlen, gw), lambda b, g: (b, g)),
                  pl.BlockSpec((seqlen, n), lambda b, g: (b, xb + g)),
                  pl.BlockSpec((seqlen, LANES), lambda b, g: (b, 0)),
                  pl.BlockSpec((1, LANES), lambda b, g: (0, 0)),
                  pl.BlockSpec((seqlen, seqlen), lambda b, g: (0, 0))],
        out_specs=pl.BlockSpec((None, 2, None, n, gw), lambda b, g: (b, 0, g, 0, 0)),
        out_shape=jax.ShapeDtypeStruct((bsz, 2, N_GROUPS, n, gw), F32),
        compiler_params=_params(("parallel", "arbitrary")),
        name=name,
    )(xbc, xbc, dt, alog, su)


def _tap_conv(src_ref, w_ref, dst_store, n_out, sub, lanes):
    taps = w_ref.shape[0]
    wk = w_ref[:, :, lanes].astype(F32)

    def step(i, carry):
        i0 = i * CONV_STEP
        win = src_ref[pl.ds(i0, taps - 1 + CONV_STEP), sub, lanes].astype(F32)
        for q in range(CONV_STEP):
            dst_store(i0 + q, jnp.sum(win[q:q + taps] * wk, axis=0))
        return carry

    lax.fori_loop(0, n_out // CONV_STEP, step, 0)


def _cconv_kernel(u_ref, wcol_ref, wrow_ref, cb_ref, lw_ref, lb_ref, o_ref,
                  rowpad_ref, col32_ref, colt_ref, ycolt_ref, yrow_ref):
    rows, wcol, d = u_ref.shape
    half = d // 2
    taps = wcol_ref.shape[0]
    pad = taps // 2

    rowpad_ref[0:pad] = jnp.zeros((pad, wcol, half), BF16)
    rowpad_ref[pad + rows:] = jnp.zeros((pad, wcol, half), BF16)
    colt_ref[0:pad] = jnp.zeros((pad, rows, half), BF16)
    colt_ref[pad + wcol:] = jnp.zeros((pad, rows, half), BF16)

    def fill(r, carry):
        rowpad_ref[pad + r] = u_ref[r, :, half:]
        col32_ref[r] = u_ref[r, :, :half].astype(F32)
        return carry

    lax.fori_loop(0, rows, fill, 0)

    def to_column_major(w, carry):
        ycolt_ref[w] = col32_ref[:, w, :]
        return carry

    lax.fori_loop(0, wcol, to_column_major, 0)

    def pack_columns(w, carry):
        colt_ref[pad + w] = ycolt_ref[w].astype(BF16)
        return carry

    lax.fori_loop(0, wcol, pack_columns, 0)

    for j in range(half // LANES):
        lanes = slice(j * LANES, (j + 1) * LANES)
        for s in range(rows // BF16_ROWS):
            sub = slice(s * BF16_ROWS, (s + 1) * BF16_ROWS)

            def store_col(w, val, sub=sub, lanes=lanes):
                ycolt_ref[w, sub, lanes] = val

            _tap_conv(colt_ref, wcol_ref, store_col, wcol, sub, lanes)
        for s in range(wcol // BF16_ROWS):
            sub = slice(s * BF16_ROWS, (s + 1) * BF16_ROWS)

            def store_row(r, val, sub=sub, lanes=lanes):
                yrow_ref[r, sub, lanes] = val

            _tap_conv(rowpad_ref, wrow_ref, store_row, rows, sub, lanes)

    def to_row_major(w, carry):
        col32_ref[:, w, :] = ycolt_ref[w]
        return carry

    lax.fori_loop(0, wcol, to_row_major, 0)

    def norm(r, carry):
        v = jnp.concatenate([col32_ref[r], yrow_ref[r]], axis=1) + cb_ref[...]
        mu = jnp.mean(v, axis=-1, keepdims=True)
        vc = v - mu
        var = jnp.mean(vc * vc, axis=-1, keepdims=True)
        yv = vc * lax.rsqrt(var + EPS) * lw_ref[...] + lb_ref[...]
        o_ref[r] = _silu(yv).astype(BF16)
        return carry

    lax.fori_loop(0, rows, norm, 0, unroll=2)


def _cconv_call(u, cw, cb, lw, lb, bsz, seqlen, name="axial_conv"):
    d = u.shape[1]
    rows = seqlen // GRID_W
    taps = cw.shape[0]
    pad = taps // 2
    half = d // 2
    assert rows % BF16_ROWS == 0 and GRID_W % BF16_ROWS == 0 and rows % CONV_STEP == 0
    wb = jnp.broadcast_to(cw.astype(BF16)[:, None, :], (taps, BF16_ROWS, d))
    grid_block = pl.BlockSpec((None, rows, GRID_W, d), lambda b: (b, 0, 0, 0))
    out = pl.pallas_call(
        _cconv_kernel,
        grid=(bsz,),
        in_specs=[grid_block, _resident((taps, BF16_ROWS, half)), _resident((taps, BF16_ROWS, half)),
                  _resident((1, d)), _resident((1, d)), _resident((1, d))],
        out_specs=grid_block,
        out_shape=jax.ShapeDtypeStruct((bsz, rows, GRID_W, d), BF16),
        scratch_shapes=[pltpu.VMEM((rows + 2 * pad, GRID_W, half), BF16),
                        pltpu.VMEM((rows, GRID_W, half), F32),
                        pltpu.VMEM((GRID_W + 2 * pad, rows, half), BF16),
                        pltpu.VMEM((GRID_W, rows, half), F32),
                        pltpu.VMEM((rows, GRID_W, half), F32)],
        compiler_params=_params(("parallel",)),
        name=name,
    )(u.reshape(bsz, rows, GRID_W, d), wb[:, :, :half], wb[:, :, half:], cb, lw, lb)
    return out.reshape(bsz * seqlen, d)


def _outproj_kernel(x_ref, y_ref, u_ref, mod_ref, wy_ref, wu_ref, o_ref):
    mix = _dot(y_ref[...], wy_ref[...]) + _dot(u_ref[...], wu_ref[...])
    o_ref[...] = x_ref[...] + mod_ref[5:6, :] * mix


def _outproj_call(x, y, u, mod, rows_per_mod, wy, wu, name="out_proj"):
    t, d = x.shape
    tm = _row_tile(t, rows_per_mod, 1024)
    tiles_per_mod = rows_per_mod // tm
    row = lambda i: (i, 0)
    return pl.pallas_call(
        _outproj_kernel,
        grid=(t // tm,),
        in_specs=[pl.BlockSpec((tm, d), row), pl.BlockSpec((tm, y.shape[1]), row), pl.BlockSpec((tm, u.shape[1]), row),
                  pl.BlockSpec((None, N_MOD, d), lambda i: (i // tiles_per_mod, 0, 0)),
                  _resident(wy.shape), _resident(wu.shape)],
        out_specs=pl.BlockSpec((tm, d), row),
        out_shape=jax.ShapeDtypeStruct((t, d), F32),
        compiler_params=_params(("parallel",)),
        name=name,
    )(x, y, u, mod, wy, wu)


def kernel(x, c, ctx, c_ctx, w_mod, b_mod, norm_ffn1, ffn1_gate, ffn1_up, ffn1_down, norm_mix, w_in, ssm_conv_w, ssm_conv_b, dt_bias_fwd, dt_bias_bwd, a_log_fwd, a_log_bwd, ssm_d, ssm_norm_w, cconv_w, cconv_b, cconv_ln_w, cconv_ln_b, w_out, norm_ffn2, ffn2_gate, ffn2_up, ffn2_down, final_norm):
    bsz, seqlen, d = x.shape
    clen = ctx.shape[1]
    depth = w_mod.shape[0]
    d_ssm = ssm_norm_w.shape[1]
    d_conv = cconv_w.shape[2]
    d_xbc = ssm_conv_w.shape[2]
    n_heads = d_ssm // HEAD_DIM
    off_x = d_ssm
    off_dt = off_x + d_xbc
    off_glu = off_dt + 2 * n_heads
    assert seqlen % CHUNK == 0 and seqlen % GRID_W == 0 and 2 * n_heads <= LANES

    def pad_lanes(v):
        return jnp.pad(v, (0, LANES - v.shape[0])).reshape(1, LANES)

    xt = x.reshape(bsz * seqlen, d)
    xc = ctx.reshape(bsz * clen, d)
    assert depth == 1, "the context stream of non-final layers is not implemented"
    c_all = jnp.concatenate([c, c_ctx[None, :], jnp.zeros((-(bsz + 1) % 8, d), F32)], axis=0)
    for i in range(depth):
        last = i == depth - 1
        mod_all = _mod_call(c_all, w_mod[i], b_mod[i].reshape(1, -1))
        mod = mod_all[:bsz].reshape(bsz, N_MOD, d)
        mod_c = mod_all[bsz:bsz + 1].reshape(1, N_MOD, d)

        wg1, wu1, wd1 = ffn1_gate[i].astype(BF16), ffn1_up[i].astype(BF16), ffn1_down[i].astype(BF16)
        nw1 = norm_ffn1[i].reshape(1, d)
        xt = _ffn_call(xt, mod, seqlen, 0, nw1, wg1, wu1, wd1, name="ffn1")
        xc = _ffn_call(xc, mod_c, bsz * clen, 0, nw1, wg1, wu1, wd1, name="ffn1_ctx")

        wi = w_in[i]
        w_cat = jnp.concatenate(
            [wi[:, :off_dt], jnp.pad(wi[:, off_dt:off_glu], ((0, 0), (0, LANES - 2 * n_heads))), wi[:, off_glu:]],
            axis=1).astype(BF16)
        dtb = pad_lanes(jnp.concatenate([dt_bias_fwd[i], dt_bias_bwd[i]]))
        alog = pad_lanes(jnp.concatenate([a_log_fwd[i], a_log_bwd[i]]))
        nwm = norm_mix[i].reshape(1, d)
        conv_b = ssm_conv_b[i].reshape(1, -1)

        zs, xbc, dt, u = _inproj_call(xt, mod, seqlen, seqlen, nwm, w_cat, dtb, ssm_conv_w[i], conv_b,
                                      d_ssm, d_xbc, d_conv, name="in_proj")
        _, xbc_c, dt_c, _ = _inproj_call(xc, mod_c, bsz * clen, clen, nwm, w_cat, dtb, ssm_conv_w[i], conv_b,
                                         d_ssm, d_xbc, d_conv, name="in_proj_ctx")
        h0 = _ctx_state_call(xbc_c, dt_c, alog, bsz, clen, d_ssm)

        dsk_e = jnp.repeat(ssm_d[i], HEAD_DIM).reshape(1, d_ssm)
        y = _ssd_call(xbc, zs, dt, alog, dsk_e, ssm_norm_w[i].reshape(1, d_ssm), h0, bsz, seqlen, d_ssm)
        uc = _cconv_call(u, cconv_w[i], cconv_b[i].reshape(1, -1), cconv_ln_w[i].reshape(1, -1),
                         cconv_ln_b[i].reshape(1, -1), bsz, seqlen)
        wo = w_out[i].astype(BF16)
        xt = _outproj_call(xt, y, uc, mod, seqlen, wo[:d_ssm], wo[d_ssm:])
        xt = _ffn_call(xt, mod, seqlen, 6, norm_ffn2[i].reshape(1, d), ffn2_gate[i].astype(BF16),
                       ffn2_up[i].astype(BF16), ffn2_down[i].astype(BF16),
                       final_w=final_norm.reshape(1, d) if last else None, name="ffn2")
    return xt.reshape(bsz, seqlen, d)
```

```python
import functools

import jax
import jax.numpy as jnp
from jax import lax
from jax.experimental import pallas as pl
from jax.experimental.pallas import tpu as pltpu

F32 = jnp.float32
BF16 = jnp.bfloat16

EPS = 1e-6
LOG2E = 1.4426950408889634
GRID_W = 64
HEAD_DIM = 64
N_GROUPS = 2
N_STATE = 128
CHUNK = 128
N_MOD = 9
LANES = 128
BF16_ROWS = 16
HALO = BF16_ROWS
CONV_STEP = 4
VMEM_LIMIT = 56 * 1024 * 1024


def _sigmoid(x):
    return 1.0 / (1.0 + jnp.exp(-x))


def _silu(x):
    return x * _sigmoid(x)


def _softplus(x):
    return jnp.maximum(x, 0.0) + jnp.log1p(jnp.exp(-jnp.abs(x)))


def _dot(a, b):
    return jnp.dot(a, b, preferred_element_type=F32)


def _split3(a):
    hi = a.astype(BF16)
    r1 = a - hi.astype(F32)
    mid = r1.astype(BF16)
    lo = (r1 - mid.astype(F32)).astype(BF16)
    return hi, mid, lo


def _dot_exact_rhs01(a, m01):
    hi, mid, lo = _split3(a)
    return _dot(hi, m01) + _dot(mid, m01) + _dot(lo, m01)


def _dot_exact_lhs01(m01, a):
    hi, mid, lo = _split3(a)
    return _dot(m01, hi) + _dot(m01, mid) + _dot(m01, lo)


def _modulated_norm(x, nw, shift, scale):
    ms = jnp.mean(x * x, axis=-1, keepdims=True)
    return (x * lax.rsqrt(ms + EPS) * nw) * (1.0 + scale) + shift


def _resident(shape):
    return pl.BlockSpec(shape, lambda *_: (0,) * len(shape), pipeline_mode=pl.Buffered(1))


def _params(sem):
    return pltpu.CompilerParams(dimension_semantics=sem, vmem_limit_bytes=VMEM_LIMIT)


def _mod_kernel(c_ref, w_ref, b_ref, o_ref):
    a = _silu(c_ref[...]).astype(BF16)
    o_ref[...] = _dot(a, w_ref[...].astype(BF16)) + b_ref[...]


def _mod_call(c_all, w, b):
    m, d = c_all.shape
    n = w.shape[1]
    tn = 1024
    return pl.pallas_call(
        _mod_kernel,
        grid=(n // tn,),
        in_specs=[pl.BlockSpec((m, d), lambda j: (0, 0)),
                  pl.BlockSpec((d, tn), lambda j: (0, j)),
                  pl.BlockSpec((1, tn), lambda j: (0, j))],
        out_specs=pl.BlockSpec((m, tn), lambda j: (0, j)),
        out_shape=jax.ShapeDtypeStruct((m, n), F32),
        compiler_params=_params(("arbitrary",)),
        name="mod_proj",
    )(c_all, w, b)


def _ffn_kernel(*refs, mod_idx, tf, has_final):
    if has_final:
        x_ref, mod_ref, nw_ref, wg_ref, wu_ref, wd_ref, fn_ref, o_ref, acc_ref = refs
    else:
        x_ref, mod_ref, nw_ref, wg_ref, wu_ref, wd_ref, o_ref, acc_ref = refs
    x = x_ref[...]
    shift = mod_ref[mod_idx:mod_idx + 1, :]
    scale = mod_ref[mod_idx + 1:mod_idx + 2, :]
    gate = mod_ref[mod_idx + 2:mod_idx + 3, :]
    h = _modulated_norm(x, nw_ref[...], shift, scale).astype(BF16)
    d_ff = wg_ref.shape[1]
    for j in range(d_ff // tf):
        g = _dot(h, wg_ref[:, j * tf:(j + 1) * tf])
        u = _dot(h, wu_ref[:, j * tf:(j + 1) * tf])
        a = (_silu(g) * u).astype(BF16)
        part = _dot(a, wd_ref[j * tf:(j + 1) * tf, :])
        if j == 0:
            acc_ref[...] = part
        else:
            acc_ref[...] += part
    out = x + (0.5 * gate) * acc_ref[...]
    if has_final:
        ms = jnp.mean(out * out, axis=-1, keepdims=True)
        out = out * lax.rsqrt(ms + EPS) * fn_ref[...]
    o_ref[...] = out


def _ffn_call(x, mod, rows_per_mod, mod_idx, nw, wg, wu, wd, final_w=None, name="ffn"):
    t, d = x.shape
    f = wg.shape[1]
    tm = _row_tile(t, rows_per_mod, 1024)
    tiles_per_mod = rows_per_mod // tm
    has_final = final_w is not None
    in_specs = [pl.BlockSpec((tm, d), lambda i: (i, 0)),
                pl.BlockSpec((None, N_MOD, d), lambda i: (i // tiles_per_mod, 0, 0)),
                _resident((1, d)), _resident((d, f)), _resident((d, f)), _resident((f, d))]
    args = [x, mod, nw, wg, wu, wd]
    if has_final:
        in_specs.append(_resident((1, d)))
        args.append(final_w)
    return pl.pallas_call(
        functools.partial(_ffn_kernel, mod_idx=mod_idx, tf=256, has_final=has_final),
        grid=(t // tm,),
        in_specs=in_specs,
        out_specs=pl.BlockSpec((tm, d), lambda i: (i, 0)),
        out_shape=jax.ShapeDtypeStruct((t, d), F32),
        scratch_shapes=[pltpu.VMEM((tm, d), F32)],
        compiler_params=_params(("parallel",)),
        name=name,
    )(*args)


def _row_tile(t, rows_per_mod, target):
    tm = min(target, rows_per_mod, t)
    while rows_per_mod % tm or t % tm:
        tm //= 2
    return tm


def _inproj_kernel(x_ref, mod_ref, nw_ref, w_ref, dtb_ref, *out_refs, d_ssm, d_xbc, d_conv, ssd_inputs_only):
    if ssd_inputs_only:
        xbc_ref, dt_ref = out_refs
    else:
        zs_ref, xbc_ref, dt_ref, u_ref = out_refs
    h = _modulated_norm(x_ref[...], nw_ref[...], mod_ref[3:4, :], mod_ref[4:5, :]).astype(BF16)
    tc = 512
    off_xbc, off_dt = d_ssm, d_ssm + d_xbc
    off_glu = off_dt + LANES
    for j in range(d_xbc // tc):
        v = _dot(h, w_ref[:, off_xbc + j * tc:off_xbc + (j + 1) * tc])
        xbc_ref[:, j * tc:(j + 1) * tc] = v.astype(BF16)
    dt_raw = _dot(h, w_ref[:, off_dt:off_dt + LANES])
    dt_ref[...] = _softplus(dt_raw + dtb_ref[...])
    if ssd_inputs_only:
        return
    for j in range(d_ssm // tc):
        z = _dot(h, w_ref[:, j * tc:(j + 1) * tc])
        zs_ref[:, j * tc:(j + 1) * tc] = _silu(z).astype(BF16)
    for j in range(d_conv // tc):
        ga = _dot(h, w_ref[:, off_glu + j * tc:off_glu + (j + 1) * tc])
        gb = _dot(h, w_ref[:, off_glu + d_conv + j * tc:off_glu + d_conv + (j + 1) * tc])
        u_ref[:, j * tc:(j + 1) * tc] = (ga * _sigmoid(gb)).astype(BF16)


def _inproj_call(x, mod, rows_per_mod, nw, w_cat, dtb, d_ssm, d_xbc, d_conv, ssd_inputs_only, name):
    t, d = x.shape
    ncat = w_cat.shape[1]
    tm = _row_tile(t, rows_per_mod, 1024)
    tiles_per_mod = rows_per_mod // tm
    row = lambda i: (i, 0)
    widths = [(d_xbc, BF16), (LANES, F32)] if ssd_inputs_only else \
        [(d_ssm, BF16), (d_xbc, BF16), (LANES, F32), (d_conv, BF16)]
    return pl.pallas_call(
        functools.partial(_inproj_kernel, d_ssm=d_ssm, d_xbc=d_xbc, d_conv=d_conv, ssd_inputs_only=ssd_inputs_only),
        grid=(t // tm,),
        in_specs=[pl.BlockSpec((tm, d), row),
                  pl.BlockSpec((None, N_MOD, d), lambda i: (i // tiles_per_mod, 0, 0)),
                  _resident((1, d)), _resident((d, ncat)), _resident((1, LANES))],
        out_specs=[pl.BlockSpec((tm, w), row) for w, _ in widths],
        out_shape=[jax.ShapeDtypeStruct((t, w), dtype) for w, dtype in widths],
        compiler_params=_params(("parallel",)),
        name=name,
    )(x, mod, nw, w_cat, dtb)


def _roll_heads_to_front(v, first_lane):
    return pltpu.roll(v, (LANES - first_lane) % LANES, axis=1)


def _head_expansion(first_lane, width):
    lane = lax.broadcasted_iota(jnp.int32, (LANES, width), 0)
    head = lax.broadcasted_iota(jnp.int32, (LANES, width), 1) // HEAD_DIM
    return jnp.where(lane == head + first_lane, 1.0, 0.0).astype(BF16)


def _dot_rhs01_2term(a, m01):
    hi = a.astype(BF16)
    lo = (a - hi.astype(F32)).astype(BF16)
    return _dot(hi, m01) + _dot(lo, m01)


def _row_shift_matrix(taps, q):
    ks = jnp.array([k for k in range(taps) if k != taps // 2])
    i = jnp.arange(q)[None, :, None]
    j = jnp.arange(q + 2 * HALO)[None, None, :]
    return (j == i + HALO - taps // 2 + ks[:, None, None]).astype(BF16).reshape((taps - 1) * q, q + 2 * HALO)


def _conv_silu_chunk(src_refs, w_refs, b_refs, shift_ref, c, nchunks, q):
    taps = w_refs[0].shape[0]
    seqlen = nchunks * q
    r0 = c * q
    if isinstance(c, int):
        lo, hi = max(r0 - HALO, 0), min(r0 + q, seqlen - HALO)
    else:
        r0 = pl.multiple_of(r0, q)
        lo = pl.multiple_of(jnp.maximum(r0 - HALO, 0), HALO)
        hi = pl.multiple_of(jnp.minimum(r0 + q, seqlen - HALO), HALO)

    def lanes(parts):
        return parts[0] if len(parts) == 1 else jnp.concatenate(parts, axis=1)

    cur = lanes([r[pl.ds(r0, q), :] for r in src_refs])
    prev = lanes([r[pl.ds(lo, HALO), :] for r in src_refs])
    nxt = lanes([r[pl.ds(hi, HALO), :] for r in src_refs])
    w = lanes([r[...] for r in w_refs])
    prev = jnp.where(c > 0, prev, jnp.zeros_like(prev))
    nxt = jnp.where(c < nchunks - 1, nxt, jnp.zeros_like(nxt))
    win = jnp.concatenate([prev, cur, nxt], axis=0)
    shifted = _dot(shift_ref[...], win)
    acc = cur.astype(F32) * w[taps // 2:taps // 2 + 1, :]
    for i, k in enumerate(k for k in range(taps) if k != taps // 2):
        acc = acc + shifted[i * q:(i + 1) * q] * w[k:k + 1, :]
    return _silu(acc + lanes([r[...] for r in b_refs]))


def _ssd_kernel(xr_ref, br_ref, cr_ref, zs_ref, dt_ref, cwx_ref, cwb_ref, cwc_ref, cbx_ref, cbb_ref, cbc_ref,
                alog_ref, dsk_ref, nw_ref, h0_ref, tl_ref, shift_ref, y_ref,
                xs_ref, cc_ref, cs_ref, vt_ref, vtall_ref, xw_ref, dec_ref, g_ref, bt_ref, yacc_ref, st_ref,
                *, seqlen, heads_per_group, n_heads):
    g = pl.program_id(1)
    q = CHUNK
    nchunks = seqlen // q
    gw = xs_ref.shape[1]
    t01 = tl_ref[...]
    rows = lax.broadcasted_iota(jnp.int32, (q, q), 0)
    cols = lax.broadcasted_iota(jnp.int32, (q, q), 1)
    lane = lax.broadcasted_iota(jnp.int32, (q, LANES), 1)
    lane_lo = lane < HEAD_DIM
    bwd_lane = lane >= n_heads
    a_all = -jnp.exp(alog_ref[...])
    first_lanes = [dirn * n_heads + g * heads_per_group for dirn in range(2)]
    e_dir = [_head_expansion(fl, gw) for fl in first_lanes]
    dsk = dsk_ref[...]
    nw = nw_ref[...]

    def prepare(c, carry):
        r0 = pl.multiple_of(c * q, q)
        dt = dt_ref[pl.ds(r0, q), :]
        da = dt * a_all
        pre = _dot_exact_lhs01(t01, da)
        tot = pre[q - 1:q, :]
        cs_all = jnp.where(bwd_lane, tot - pre + da, pre)
        k_all = dt * jnp.exp(tot - cs_all)
        vtall = vtall_ref.at[c % 2]
        vtall[...] = ((cs_all - jnp.log(dt)) * LOG2E).T
        dec_all = jnp.broadcast_to(jnp.exp(tot), (8, LANES))
        cs2 = cs_all * LOG2E
        x = _conv_silu_chunk([xr_ref], [cwx_ref], [cbx_ref], shift_ref, c, nchunks, q)
        xs_ref[pl.ds(r0, q), :] = x.astype(BF16)
        bc = _conv_silu_chunk([br_ref, cr_ref], [cwb_ref, cwc_ref], [cbb_ref, cbc_ref], shift_ref, c, nchunks, q)
        bt = bc[:, :N_STATE].T.astype(BF16)
        bt_ref[c] = bt
        cc = bc[:, N_STATE:].astype(BF16)
        cc_ref[pl.ds(r0, q), :] = cc
        g_ref[c] = _dot(cc, bt)
        for dirn in range(2):
            fl = first_lanes[dirn]
            cs_ref[dirn, pl.ds(r0, q), :] = _roll_heads_to_front(cs2, fl)
            vt_ref[dirn, c] = vtall[pl.ds(pl.multiple_of(fl, 8), 8), :]
            xw_ref[dirn, pl.ds(r0, q), :] = (x * _dot_rhs01_2term(k_all, e_dir[dirn])).astype(BF16)
            dec_ref[dirn, c] = _dot_exact_rhs01(dec_all, e_dir[dirn])
        return carry

    lax.fori_loop(0, nchunks, prepare, 0, unroll=2)

    def scan_chunk(c, dirn):
        r0 = pl.multiple_of(c * q, q)
        cs = cs_ref[dirn, pl.ds(r0, q), :]
        vt = vt_ref[dirn, c]
        gmat = g_ref[c]
        cc32 = cc_ref[pl.ds(r0, q), :].astype(F32)
        x_b = xs_ref[pl.ds(r0, q), :]
        st = st_ref[dirn]
        st_b = st.astype(BF16)
        mask = (rows >= cols) if dirn == 0 else (rows <= cols)
        ys = []
        for p in range(heads_per_group // 2):
            sl = slice(p * LANES, (p + 1) * LANES)
            rhs = jnp.concatenate([x_b[:, sl], st_b[:, sl]], axis=0)
            lhs = []
            for hh in range(2):
                h = 2 * p + hh
                cs_col = jnp.broadcast_to(cs[:, h:h + 1], (q, LANES))
                decay = jnp.exp2(jnp.where(mask, cs_col - vt[h:h + 1, :], -jnp.inf))
                m = (gmat * decay).astype(BF16)
                ce = (cc32 * jnp.exp2(cs_col)).astype(BF16)
                lhs.append(jnp.concatenate([m, ce], axis=1))
            res = _dot(jnp.concatenate(lhs, axis=0), rhs)
            ys.append(jnp.where(lane_lo, res[:q], res[q:]))
        st_ref[dirn] = dec_ref[dirn, c][0:1, :] * st + _dot(bt_ref[c], xw_ref[dirn, pl.ds(r0, q), :])
        return jnp.concatenate(ys, axis=1)

    def finish(c, y):
        r0 = pl.multiple_of(c * q, q)
        y = y + yacc_ref[pl.ds(r0, q), :] + dsk * xs_ref[pl.ds(r0, q), :].astype(F32)
        y = y * zs_ref[pl.ds(r0, q), :].astype(F32)
        ms = jnp.mean(y * y, axis=-1, keepdims=True)
        y_ref[pl.ds(r0, q), :] = (y * lax.rsqrt(ms + EPS) * nw).astype(BF16)

    st_ref[...] = h0_ref[...]
    half = nchunks // 2

    def first_half(i, carry):
        cf, cb = i, nchunks - 1 - i
        yacc_ref[pl.ds(pl.multiple_of(cf * q, q), q), :] = scan_chunk(cf, 0)
        yacc_ref[pl.ds(pl.multiple_of(cb * q, q), q), :] = scan_chunk(cb, 1)
        return carry

    def second_half(i, carry):
        cf, cb = i, nchunks - 1 - i
        finish(cf, scan_chunk(cf, 0))
        finish(cb, scan_chunk(cb, 1))
        return carry

    lax.fori_loop(0, half, first_half, 0)
    lax.fori_loop(half, nchunks, second_half, 0)


def _ssd_call(xbc, zs, dt, conv_w, conv_b, alog, dsk_e, norm_w, h0, bsz, seqlen, d_ssm, name="ssd_scan"):
    n = N_STATE
    q = CHUNK
    gw = d_ssm // N_GROUPS
    hpg = gw // HEAD_DIM
    n_heads = d_ssm // HEAD_DIM
    xb = d_ssm // n
    cb = xb + N_GROUPS
    nchunks = seqlen // q
    taps = conv_w.shape[0]
    assert nchunks % 2 == 0 and hpg % 8 == 0 and taps // 2 <= HALO
    tl = (jnp.arange(q)[:, None] >= jnp.arange(q)[None, :]).astype(BF16)
    return pl.pallas_call(
        functools.partial(_ssd_kernel, seqlen=seqlen, heads_per_group=hpg, n_heads=n_heads),
        grid=(bsz, N_GROUPS),
        in_specs=[pl.BlockSpec((seqlen, gw), lambda b, g: (b, g)),
                  pl.BlockSpec((seqlen, n), lambda b, g: (b, xb + g)),
                  pl.BlockSpec((seqlen, n), lambda b, g: (b, cb + g)),
                  pl.BlockSpec((seqlen, gw), lambda b, g: (b, g)),
                  pl.BlockSpec((seqlen, LANES), lambda b, g: (b, 0)),
                  pl.BlockSpec((taps, gw), lambda b, g: (0, g)),
                  pl.BlockSpec((taps, n), lambda b, g: (0, xb + g)),
                  pl.BlockSpec((taps, n), lambda b, g: (0, cb + g)),
                  pl.BlockSpec((1, gw), lambda b, g: (0, g)),
                  pl.BlockSpec((1, n), lambda b, g: (0, xb + g)),
                  pl.BlockSpec((1, n), lambda b, g: (0, cb + g)),
                  pl.BlockSpec((1, LANES), lambda b, g: (0, 0)),
                  pl.BlockSpec((1, gw), lambda b, g: (0, g)),
                  pl.BlockSpec((1, gw), lambda b, g: (0, g)),
                  pl.BlockSpec((None, 2, None, n, gw), lambda b, g: (b, 0, g, 0, 0)),
                  pl.BlockSpec((q, q), lambda b, g: (0, 0)),
                  pl.BlockSpec(((taps - 1) * q, q + 2 * HALO), lambda b, g: (0, 0))],
        out_specs=pl.BlockSpec((seqlen, gw), lambda b, g: (b, g)),
        out_shape=jax.ShapeDtypeStruct((bsz * seqlen, d_ssm), BF16),
        scratch_shapes=[pltpu.VMEM((seqlen, gw), BF16),
                        pltpu.VMEM((seqlen, n), BF16),
                        pltpu.VMEM((2, seqlen, LANES), F32),
                        pltpu.VMEM((2, nchunks, 8, q), F32),
                        pltpu.VMEM((2, LANES, q), F32),
                        pltpu.VMEM((2, seqlen, gw), BF16),
                        pltpu.VMEM((2, nchunks, 8, gw), F32),
                        pltpu.VMEM((nchunks, q, q), F32),
                        pltpu.VMEM((nchunks, n, q), BF16),
                        pltpu.VMEM((seqlen, gw), F32),
                        pltpu.VMEM((2, n, gw), F32)],
        compiler_params=_params(("parallel", "arbitrary")),
        name=name,
    )(xbc, xbc, xbc, zs, dt, conv_w, conv_w, conv_w, conv_b, conv_b, conv_b, alog, dsk_e, norm_w, h0, tl,
      _row_shift_matrix(taps, q))


def _ctx_state_kernel(xr_ref, br_ref, dt_ref, cwx_ref, cwb_ref, cbx_ref, cbb_ref, alog_ref, su_ref, shift_ref,
                      h0_ref, *, heads_per_group, n_heads):
    g = pl.program_id(1)
    seqlen, gw = xr_ref.shape
    q = CHUNK
    nchunks = seqlen // q
    xb = jnp.concatenate([_conv_silu_chunk([xr_ref, br_ref], [cwx_ref, cwb_ref], [cbx_ref, cbb_ref], shift_ref,
                                           c, nchunks, q) for c in range(nchunks)], axis=0)
    x, b = xb[:, :gw], xb[:, gw:]
    b_t = b.T.astype(BF16)
    dt = dt_ref[...]
    da = dt * -jnp.exp(alog_ref[...])
    later = _dot_exact_lhs01(su_ref[...], da)
    total = later[0:1, :] + da[0:1, :]
    bwd_lane = lax.broadcasted_iota(jnp.int32, (seqlen, LANES), 1) >= n_heads
    rest = jnp.where(bwd_lane, total - later - da, later)
    wgt = jnp.exp(rest) * dt
    for dirn in range(2):
        e01 = _head_expansion(dirn * n_heads + g * heads_per_group, gw)
        xw = (x * _dot_rhs01_2term(wgt, e01)).astype(BF16)
        h0_ref[dirn] = _dot(b_t, xw)


def _ctx_state_call(xbc, dt, conv_w, conv_b, alog, bsz, seqlen, d_ssm, name="ctx_states"):
    n = N_STATE
    gw = d_ssm // N_GROUPS
    hpg = gw // HEAD_DIM
    n_heads = d_ssm // HEAD_DIM
    xb = d_ssm // n
    su = (jnp.arange(seqlen)[:, None] < jnp.arange(seqlen)[None, :]).astype(BF16)
    taps = conv_w.shape[0]
    assert seqlen % CHUNK == 0
    return pl.pallas_call(
        functools.partial(_ctx_state_kernel, heads_per_group=hpg, n_heads=n_heads),
        grid=(bsz, N_GROUPS),
        in_specs=[pl.BlockSpec((seqlen, gw), lambda b, g: (b, g)),
                  pl.BlockSpec((seqlen, n), lambda b, g: (b, xb + g)),
                  pl.BlockSpec((seqlen, LANES), lambda b, g: (b, 0)),
                  pl.BlockSpec((taps, gw), lambda b, g: (0, g)),
                  pl.BlockSpec((taps, n), lambda b, g: (0, xb + g)),
                  pl.BlockSpec((1, gw), lambda b, g: (0, g)),
                  pl.BlockSpec((1, n), lambda b, g: (0, xb + g)),
                  pl.BlockSpec((1, LANES), lambda b, g: (0, 0)),
                  pl.BlockSpec((seqlen, seqlen), lambda b, g: (0, 0)),
                  pl.BlockSpec(((taps - 1) * CHUNK, CHUNK + 2 * HALO), lambda b, g: (0, 0))],
        out_specs=pl.BlockSpec((None, 2, None, n, gw), lambda b, g: (b, 0, g, 0, 0)),
        out_shape=jax.ShapeDtypeStruct((bsz, 2, N_GROUPS, n, gw), F32),
        compiler_params=_params(("parallel", "arbitrary")),
        name=name,
    )(xbc, xbc, dt, conv_w, conv_w, conv_b, conv_b, alog, su, _row_shift_matrix(taps, CHUNK))


def _tap_conv(src_ref, w_ref, dst_store, n_out, sub, lanes):
    taps = w_ref.shape[0]
    wk = w_ref[:, :, lanes].astype(F32)

    def step(i, carry):
        i0 = i * CONV_STEP
        win = src_ref[pl.ds(i0, taps - 1 + CONV_STEP), sub, lanes].astype(F32)
        for q in range(CONV_STEP):
            dst_store(i0 + q, jnp.sum(win[q:q + taps] * wk, axis=0))
        return carry

    lax.fori_loop(0, n_out // CONV_STEP, step, 0)


def _cconv_kernel(u_ref, wcol_ref, wrow_ref, cb_ref, lw_ref, lb_ref, o_ref,
                  rowpad_ref, col32_ref, colt_ref, ycolt_ref, yrow_ref):
    rows, wcol, d = u_ref.shape
    half = d // 2
    taps = wcol_ref.shape[0]
    pad = taps // 2

    rowpad_ref[0:pad] = jnp.zeros((pad, wcol, half), BF16)
    rowpad_ref[pad + rows:] = jnp.zeros((pad, wcol, half), BF16)
    colt_ref[0:pad] = jnp.zeros((pad, rows, half), BF16)
    colt_ref[pad + wcol:] = jnp.zeros((pad, rows, half), BF16)

    def fill(r, carry):
        rowpad_ref[pad + r] = u_ref[r, :, half:]
        col32_ref[r] = u_ref[r, :, :half].astype(F32)
        return carry

    lax.fori_loop(0, rows, fill, 0)

    def to_column_major(w, carry):
        ycolt_ref[w] = col32_ref[:, w, :]
        return carry

    lax.fori_loop(0, wcol, to_column_major, 0)

    def pack_columns(w, carry):
        colt_ref[pad + w] = ycolt_ref[w].astype(BF16)
        return carry

    lax.fori_loop(0, wcol, pack_columns, 0)

    for j in range(half // LANES):
        lanes = slice(j * LANES, (j + 1) * LANES)
        for s in range(rows // BF16_ROWS):
            sub = slice(s * BF16_ROWS, (s + 1) * BF16_ROWS)

            def store_col(w, val, sub=sub, lanes=lanes):
                ycolt_ref[w, sub, lanes] = val

            _tap_conv(colt_ref, wcol_ref, store_col, wcol, sub, lanes)
        for s in range(wcol // BF16_ROWS):
            sub = slice(s * BF16_ROWS, (s + 1) * BF16_ROWS)

            def store_row(r, val, sub=sub, lanes=lanes):
                yrow_ref[r, sub, lanes] = val

            _tap_conv(rowpad_ref, wrow_ref, store_row, rows, sub, lanes)

    def to_row_major(w, carry):
        col32_ref[:, w, :] = ycolt_ref[w]
        return carry

    lax.fori_loop(0, wcol, to_row_major, 0)

    def norm(r, carry):
        v = jnp.concatenate([col32_ref[r], yrow_ref[r]], axis=1) + cb_ref[...]
        mu = jnp.mean(v, axis=-1, keepdims=True)
        vc = v - mu
        var = jnp.mean(vc * vc, axis=-1, keepdims=True)
        yv = vc * lax.rsqrt(var + EPS) * lw_ref[...] + lb_ref[...]
        o_ref[r] = _silu(yv).astype(BF16)
        return carry

    lax.fori_loop(0, rows, norm, 0, unroll=2)


def _cconv_call(u, cw, cb, lw, lb, bsz, seqlen, name="axial_conv"):
    d = u.shape[1]
    rows = seqlen // GRID_W
    taps = cw.shape[0]
    pad = taps // 2
    half = d // 2
    assert rows % BF16_ROWS == 0 and GRID_W % BF16_ROWS == 0 and rows % CONV_STEP == 0
    wb = jnp.broadcast_to(cw.astype(BF16)[:, None, :], (taps, BF16_ROWS, d))
    grid_block = pl.BlockSpec((None, rows, GRID_W, d), lambda b: (b, 0, 0, 0))
    out = pl.pallas_call(
        _cconv_kernel,
        grid=(bsz,),
        in_specs=[grid_block, _resident((taps, BF16_ROWS, half)), _resident((taps, BF16_ROWS, half)),
                  _resident((1, d)), _resident((1, d)), _resident((1, d))],
        out_specs=grid_block,
        out_shape=jax.ShapeDtypeStruct((bsz, rows, GRID_W, d), BF16),
        scratch_shapes=[pltpu.VMEM((rows + 2 * pad, GRID_W, half), BF16),
                        pltpu.VMEM((rows, GRID_W, half), F32),
                        pltpu.VMEM((GRID_W + 2 * pad, rows, half), BF16),
                        pltpu.VMEM((GRID_W, rows, half), F32),
                        pltpu.VMEM((rows, GRID_W, half), F32)],
        compiler_params=_params(("parallel",)),
        name=name,
    )(u.reshape(bsz, rows, GRID_W, d), wb[:, :, :half], wb[:, :, half:], cb, lw, lb)
    return out.reshape(bsz * seqlen, d)


def _outproj_kernel(x_ref, y_ref, u_ref, mod_ref, wy_ref, wu_ref, o_ref):
    mix = _dot(y_ref[...], wy_ref[...]) + _dot(u_ref[...], wu_ref[...])
    o_ref[...] = x_ref[...] + mod_ref[5:6, :] * mix


def _outproj_call(x, y, u, mod, rows_per_mod, wy, wu, name="out_proj"):
    t, d = x.shape
    tm = _row_tile(t, rows_per_mod, 1024)
    tiles_per_mod = rows_per_mod // tm
    row = lambda i: (i, 0)
    return pl.pallas_call(
        _outproj_kernel,
        grid=(t // tm,),
        in_specs=[pl.BlockSpec((tm, d), row), pl.BlockSpec((tm, y.shape[1]), row), pl.BlockSpec((tm, u.shape[1]), row),
                  pl.BlockSpec((None, N_MOD, d), lambda i: (i // tiles_per_mod, 0, 0)),
                  _resident(wy.shape), _resident(wu.shape)],
        out_specs=pl.BlockSpec((tm, d), row),
        out_shape=jax.ShapeDtypeStruct((t, d), F32),
        compiler_params=_params(("parallel",)),
        name=name,
    )(x, y, u, mod, wy, wu)


def kernel(x, c, ctx, c_ctx, w_mod, b_mod, norm_ffn1, ffn1_gate, ffn1_up, ffn1_down, norm_mix, w_in, ssm_conv_w, ssm_conv_b, dt_bias_fwd, dt_bias_bwd, a_log_fwd, a_log_bwd, ssm_d, ssm_norm_w, cconv_w, cconv_b, cconv_ln_w, cconv_ln_b, w_out, norm_ffn2, ffn2_gate, ffn2_up, ffn2_down, final_norm):
    bsz, seqlen, d = x.shape
    clen = ctx.shape[1]
    depth = w_mod.shape[0]
    d_ssm = ssm_norm_w.shape[1]
    d_conv = cconv_w.shape[2]
    d_xbc = ssm_conv_w.shape[2]
    n_heads = d_ssm // HEAD_DIM
    off_x = d_ssm
    off_dt = off_x + d_xbc
    off_glu = off_dt + 2 * n_heads
    assert seqlen % CHUNK == 0 and seqlen % GRID_W == 0 and 2 * n_heads <= LANES

    def pad_lanes(v):
        return jnp.pad(v, (0, LANES - v.shape[0])).reshape(1, LANES)

    xt = x.reshape(bsz * seqlen, d)
    xc = ctx.reshape(bsz * clen, d)
    assert depth == 1, "the context stream of non-final layers is not implemented"
    c_all = jnp.concatenate([c, c_ctx[None, :], jnp.zeros((-(bsz + 1) % 8, d), F32)], axis=0)
    for i in range(depth):
        last = i == depth - 1
        mod_all = _mod_call(c_all, w_mod[i], b_mod[i].reshape(1, -1))
        mod = mod_all[:bsz].reshape(bsz, N_MOD, d)
        mod_c = mod_all[bsz:bsz + 1].reshape(1, N_MOD, d)

        wg1, wu1, wd1 = ffn1_gate[i].astype(BF16), ffn1_up[i].astype(BF16), ffn1_down[i].astype(BF16)
        nw1 = norm_ffn1[i].reshape(1, d)
        xt = _ffn_call(xt, mod, seqlen, 0, nw1, wg1, wu1, wd1, name="ffn1")
        xc = _ffn_call(xc, mod_c, bsz * clen, 0, nw1, wg1, wu1, wd1, name="ffn1_ctx")

        wi = w_in[i]
        w_cat = jnp.concatenate(
            [wi[:, :off_dt], jnp.pad(wi[:, off_dt:off_glu], ((0, 0), (0, LANES - 2 * n_heads))), wi[:, off_glu:]],
            axis=1).astype(BF16)
        dtb = pad_lanes(jnp.concatenate([dt_bias_fwd[i], dt_bias_bwd[i]]))
        alog = pad_lanes(jnp.concatenate([a_log_fwd[i], a_log_bwd[i]]))
        nwm = norm_mix[i].reshape(1, d)
        conv_b = ssm_conv_b[i].reshape(1, -1)

        zs, xbc, dt, u = _inproj_call(xt, mod, seqlen, nwm, w_cat, dtb, d_ssm, d_xbc, d_conv, False, name="in_proj")
        xbc_c, dt_c = _inproj_call(xc, mod_c, bsz * clen, nwm, w_cat, dtb, d_ssm, d_xbc, d_conv, True,
                                   name="in_proj_ctx")
        h0 = _ctx_state_call(xbc_c, dt_c, ssm_conv_w[i], conv_b, alog, bsz, clen, d_ssm)

        dsk_e = jnp.repeat(ssm_d[i], HEAD_DIM).reshape(1, d_ssm)
        y = _ssd_call(xbc, zs, dt, ssm_conv_w[i], conv_b, alog, dsk_e, ssm_norm_w[i].reshape(1, d_ssm), h0,
                      bsz, seqlen, d_ssm)
        uc = _cconv_call(u, cconv_w[i], cconv_b[i].reshape(1, -1), cconv_ln_w[i].reshape(1, -1),
                         cconv_ln_b[i].reshape(1, -1), bsz, seqlen)
        wo = w_out[i].astype(BF16)
        xt = _outproj_call(xt, y, uc, mod, seqlen, wo[:d_ssm], wo[d_ssm:])
        xt = _ffn_call(xt, mod, seqlen, 6, norm_ffn2[i].reshape(1, d), ffn2_gate[i].astype(BF16),
                       ffn2_up[i].astype(BF16), ffn2_down[i].astype(BF16),
                       final_w=final_norm.reshape(1, d) if last else None, name="ffn2")
    return xt.reshape(bsz, seqlen, d)
```

```python
import functools

import jax
import jax.numpy as jnp
from jax import lax
from jax.experimental import pallas as pl
from jax.experimental.pallas import tpu as pltpu

F32 = jnp.float32
BF16 = jnp.bfloat16

EPS = 1e-6
LOG2E = 1.4426950408889634
GRID_W = 64
HEAD_DIM = 64
N_GROUPS = 2
N_STATE = 128
CHUNK = 128
N_MOD = 9
LANES = 128
BF16_ROWS = 16
HALO = BF16_ROWS
CONV_STEP = 4
VMEM_LIMIT = 56 * 1024 * 1024


def _sigmoid(x):
    return 1.0 / (1.0 + jnp.exp(-x))


def _silu(x):
    return x * _sigmoid(x)


def _softplus(x):
    return jnp.maximum(x, 0.0) + jnp.log1p(jnp.exp(-jnp.abs(x)))


def _dot(a, b):
    return jnp.dot(a, b, preferred_element_type=F32)


def _split3(a):
    hi = a.astype(BF16)
    r1 = a - hi.astype(F32)
    mid = r1.astype(BF16)
    lo = (r1 - mid.astype(F32)).astype(BF16)
    return hi, mid, lo


def _dot_exact_rhs01(a, m01):
    hi, mid, lo = _split3(a)
    return _dot(hi, m01) + _dot(mid, m01) + _dot(lo, m01)


def _dot_exact_lhs01(m01, a):
    hi, mid, lo = _split3(a)
    return _dot(m01, hi) + _dot(m01, mid) + _dot(m01, lo)


def _modulated_norm(x, nw, shift, scale):
    ms = jnp.mean(x * x, axis=-1, keepdims=True)
    return (x * lax.rsqrt(ms + EPS) * nw) * (1.0 + scale) + shift


def _resident(shape):
    return pl.BlockSpec(shape, lambda *_: (0,) * len(shape), pipeline_mode=pl.Buffered(1))


def _params(sem):
    return pltpu.CompilerParams(dimension_semantics=sem, vmem_limit_bytes=VMEM_LIMIT)


def _mod_kernel(c_ref, w_ref, b_ref, o_ref):
    a = _silu(c_ref[...]).astype(BF16)
    o_ref[...] = _dot(a, w_ref[...].astype(BF16)) + b_ref[...]


def _mod_call(c_all, w, b):
    m, d = c_all.shape
    n = w.shape[1]
    tn = 1024
    return pl.pallas_call(
        _mod_kernel,
        grid=(n // tn,),
        in_specs=[pl.BlockSpec((m, d), lambda j: (0, 0)),
                  pl.BlockSpec((d, tn), lambda j: (0, j)),
                  pl.BlockSpec((1, tn), lambda j: (0, j))],
        out_specs=pl.BlockSpec((m, tn), lambda j: (0, j)),
        out_shape=jax.ShapeDtypeStruct((m, n), F32),
        compiler_params=_params(("arbitrary",)),
        name="mod_proj",
    )(c_all, w, b)


def _ffn_kernel(*refs, mod_idx, tf, has_final):
    if has_final:
        x_ref, mod_ref, nw_ref, wg_ref, wu_ref, wd_ref, fn_ref, o_ref, acc_ref = refs
    else:
        x_ref, mod_ref, nw_ref, wg_ref, wu_ref, wd_ref, o_ref, acc_ref = refs
    x = x_ref[...]
    shift = mod_ref[mod_idx:mod_idx + 1, :]
    scale = mod_ref[mod_idx + 1:mod_idx + 2, :]
    gate = mod_ref[mod_idx + 2:mod_idx + 3, :]
    h = _modulated_norm(x, nw_ref[...], shift, scale).astype(BF16)
    d_ff = wg_ref.shape[1]
    for j in range(d_ff // tf):
        g = _dot(h, wg_ref[:, j * tf:(j + 1) * tf])
        u = _dot(h, wu_ref[:, j * tf:(j + 1) * tf])
        a = (_silu(g) * u).astype(BF16)
        part = _dot(a, wd_ref[j * tf:(j + 1) * tf, :])
        if j == 0:
            acc_ref[...] = part
        else:
            acc_ref[...] += part
    out = x + (0.5 * gate) * acc_ref[...]
    if has_final:
        ms = jnp.mean(out * out, axis=-1, keepdims=True)
        out = out * lax.rsqrt(ms + EPS) * fn_ref[...]
    o_ref[...] = out


def _ffn_call(x, mod, rows_per_mod, mod_idx, nw, wg, wu, wd, final_w=None, name="ffn"):
    t, d = x.shape
    f = wg.shape[1]
    tm = _row_tile(t, rows_per_mod, 1024)
    tiles_per_mod = rows_per_mod // tm
    has_final = final_w is not None
    in_specs = [pl.BlockSpec((tm, d), lambda i: (i, 0)),
                pl.BlockSpec((None, N_MOD, d), lambda i: (i // tiles_per_mod, 0, 0)),
                _resident((1, d)), _resident((d, f)), _resident((d, f)), _resident((f, d))]
    args = [x, mod, nw, wg, wu, wd]
    if has_final:
        in_specs.append(_resident((1, d)))
        args.append(final_w)
    return pl.pallas_call(
        functools.partial(_ffn_kernel, mod_idx=mod_idx, tf=256, has_final=has_final),
        grid=(t // tm,),
        in_specs=in_specs,
        out_specs=pl.BlockSpec((tm, d), lambda i: (i, 0)),
        out_shape=jax.ShapeDtypeStruct((t, d), F32),
        scratch_shapes=[pltpu.VMEM((tm, d), F32)],
        compiler_params=_params(("parallel",)),
        name=name,
    )(*args)


def _row_tile(t, rows_per_mod, target):
    tm = min(target, rows_per_mod, t)
    while rows_per_mod % tm or t % tm:
        tm //= 2
    return tm


def _inproj_kernel(x_ref, mod_ref, nw_ref, w_ref, dtb_ref, *out_refs, d_ssm, d_xbc, d_conv, ssd_inputs_only):
    if ssd_inputs_only:
        xbc_ref, dt_ref = out_refs
    else:
        zs_ref, xbc_ref, dt_ref, u_ref = out_refs
    h = _modulated_norm(x_ref[...], nw_ref[...], mod_ref[3:4, :], mod_ref[4:5, :]).astype(BF16)
    tc = 512
    off_xbc, off_dt = d_ssm, d_ssm + d_xbc
    off_glu = off_dt + LANES
    for j in range(d_xbc // tc):
        v = _dot(h, w_ref[:, off_xbc + j * tc:off_xbc + (j + 1) * tc])
        xbc_ref[:, j * tc:(j + 1) * tc] = v.astype(BF16)
    dt_raw = _dot(h, w_ref[:, off_dt:off_dt + LANES])
    dt_ref[...] = _softplus(dt_raw + dtb_ref[...])
    if ssd_inputs_only:
        return
    for j in range(d_ssm // tc):
        z = _dot(h, w_ref[:, j * tc:(j + 1) * tc])
        zs_ref[:, j * tc:(j + 1) * tc] = _silu(z).astype(BF16)
    for j in range(d_conv // tc):
        ga = _dot(h, w_ref[:, off_glu + j * tc:off_glu + (j + 1) * tc])
        gb = _dot(h, w_ref[:, off_glu + d_conv + j * tc:off_glu + d_conv + (j + 1) * tc])
        u_ref[:, j * tc:(j + 1) * tc] = (ga * _sigmoid(gb)).astype(BF16)


def _inproj_call(x, mod, rows_per_mod, nw, w_cat, dtb, d_ssm, d_xbc, d_conv, ssd_inputs_only, name):
    t, d = x.shape
    ncat = w_cat.shape[1]
    tm = _row_tile(t, rows_per_mod, 1024)
    tiles_per_mod = rows_per_mod // tm
    row = lambda i: (i, 0)
    widths = [(d_xbc, BF16), (LANES, F32)] if ssd_inputs_only else \
        [(d_ssm, BF16), (d_xbc, BF16), (LANES, F32), (d_conv, BF16)]
    return pl.pallas_call(
        functools.partial(_inproj_kernel, d_ssm=d_ssm, d_xbc=d_xbc, d_conv=d_conv, ssd_inputs_only=ssd_inputs_only),
        grid=(t // tm,),
        in_specs=[pl.BlockSpec((tm, d), row),
                  pl.BlockSpec((None, N_MOD, d), lambda i: (i // tiles_per_mod, 0, 0)),
                  _resident((1, d)), _resident((d, ncat)), _resident((1, LANES))],
        out_specs=[pl.BlockSpec((tm, w), row) for w, _ in widths],
        out_shape=[jax.ShapeDtypeStruct((t, w), dtype) for w, dtype in widths],
        compiler_params=_params(("parallel",)),
        name=name,
    )(x, mod, nw, w_cat, dtb)


def _roll_heads_to_front(v, first_lane):
    return pltpu.roll(v, (LANES - first_lane) % LANES, axis=1)


def _head_expansion(first_lane, width):
    lane = lax.broadcasted_iota(jnp.int32, (LANES, width), 0)
    head = lax.broadcasted_iota(jnp.int32, (LANES, width), 1) // HEAD_DIM
    return jnp.where(lane == head + first_lane, 1.0, 0.0).astype(BF16)


def _dot_rhs01_2term(a, m01):
    hi = a.astype(BF16)
    lo = (a - hi.astype(F32)).astype(BF16)
    return _dot(hi, m01) + _dot(lo, m01)


def _row_shift_matrix(taps, q):
    ks = jnp.array([k for k in range(taps) if k != taps // 2])
    i = jnp.arange(q)[None, :, None]
    j = jnp.arange(q + 2 * HALO)[None, None, :]
    return (j == i + HALO - taps // 2 + ks[:, None, None]).astype(BF16).reshape((taps - 1) * q, q + 2 * HALO)


def _conv_silu_chunk(src_refs, w_refs, b_refs, shift_ref, c, nchunks, q):
    taps = w_refs[0].shape[0]
    seqlen = nchunks * q
    r0 = c * q
    if isinstance(c, int):
        lo, hi = max(r0 - HALO, 0), min(r0 + q, seqlen - HALO)
    else:
        r0 = pl.multiple_of(r0, q)
        lo = pl.multiple_of(jnp.maximum(r0 - HALO, 0), HALO)
        hi = pl.multiple_of(jnp.minimum(r0 + q, seqlen - HALO), HALO)

    def lanes(parts):
        return parts[0] if len(parts) == 1 else jnp.concatenate(parts, axis=1)

    cur = lanes([r[pl.ds(r0, q), :] for r in src_refs])
    prev = lanes([r[pl.ds(lo, HALO), :] for r in src_refs])
    nxt = lanes([r[pl.ds(hi, HALO), :] for r in src_refs])
    w = lanes([r[...] for r in w_refs])
    prev = jnp.where(c > 0, prev, jnp.zeros_like(prev))
    nxt = jnp.where(c < nchunks - 1, nxt, jnp.zeros_like(nxt))
    win = jnp.concatenate([prev, cur, nxt], axis=0)
    shifted = _dot(shift_ref[...], win)
    acc = cur.astype(F32) * w[taps // 2:taps // 2 + 1, :]
    for i, k in enumerate(k for k in range(taps) if k != taps // 2):
        acc = acc + shifted[i * q:(i + 1) * q] * w[k:k + 1, :]
    return _silu(acc + lanes([r[...] for r in b_refs]))


def _ssd_kernel(xr_ref, br_ref, cr_ref, zs_ref, dt_ref, cwx_ref, cwb_ref, cwc_ref, cbx_ref, cbb_ref, cbc_ref,
                alog_ref, dsk_ref, nw_ref, h0_ref, tl_ref, shift_ref, y_ref,
                xs_ref, cc_ref, cs_ref, vt_ref, vtall_ref, xw_ref, dec_ref, g_ref, bt_ref, yacc_ref, st_ref,
                *, seqlen, heads_per_group, n_heads):
    g = pl.program_id(1)
    q = CHUNK
    nchunks = seqlen // q
    gw = xs_ref.shape[1]
    t01 = tl_ref[...]
    rows = lax.broadcasted_iota(jnp.int32, (q, q), 0)
    cols = lax.broadcasted_iota(jnp.int32, (q, q), 1)
    lane = lax.broadcasted_iota(jnp.int32, (q, LANES), 1)
    lane_lo = lane < HEAD_DIM
    bwd_lane = lane >= n_heads
    a_all = -jnp.exp(alog_ref[...])
    first_lanes = [dirn * n_heads + g * heads_per_group for dirn in range(2)]
    e_dir = [_head_expansion(fl, gw) for fl in first_lanes]
    dsk = dsk_ref[...]
    nw = nw_ref[...]

    def prepare(c):
        r0 = pl.multiple_of(c * q, q)
        dt = dt_ref[pl.ds(r0, q), :]
        da = dt * a_all
        pre = _dot_exact_lhs01(t01, da)
        tot = pre[q - 1:q, :]
        cs_all = jnp.where(bwd_lane, tot - pre + da, pre)
        k_all = dt * jnp.exp(tot - cs_all)
        vtall = vtall_ref.at[c % 2]
        vtall[...] = ((cs_all - jnp.log(dt)) * LOG2E).T
        dec_all = jnp.broadcast_to(jnp.exp(tot), (8, LANES))
        cs2 = cs_all * LOG2E
        x = _conv_silu_chunk([xr_ref], [cwx_ref], [cbx_ref], shift_ref, c, nchunks, q)
        xs_ref[pl.ds(r0, q), :] = x.astype(BF16)
        bc = _conv_silu_chunk([br_ref, cr_ref], [cwb_ref, cwc_ref], [cbb_ref, cbc_ref], shift_ref, c, nchunks, q)
        bt = bc[:, :N_STATE].T.astype(BF16)
        bt_ref[c] = bt
        cc = bc[:, N_STATE:].astype(BF16)
        cc_ref[pl.ds(r0, q), :] = cc
        g_ref[c] = _dot(cc, bt)
        for dirn in range(2):
            fl = first_lanes[dirn]
            cs_ref[dirn, pl.ds(r0, q), :] = _roll_heads_to_front(cs2, fl)
            vt_ref[dirn, c] = vtall[pl.ds(pl.multiple_of(fl, 8), 8), :]
            xw_ref[dirn, pl.ds(r0, q), :] = (x * _dot_rhs01_2term(k_all, e_dir[dirn])).astype(BF16)
            dec_ref[dirn, c] = _dot_exact_rhs01(dec_all, e_dir[dirn])

    def scan_chunk(c, dirn):
        r0 = pl.multiple_of(c * q, q)
        cs = cs_ref[dirn, pl.ds(r0, q), :]
        vt = vt_ref[dirn, c]
        gmat = g_ref[c]
        cc32 = cc_ref[pl.ds(r0, q), :].astype(F32)
        x_b = xs_ref[pl.ds(r0, q), :]
        st = st_ref[dirn]
        st_b = st.astype(BF16)
        mask = (rows >= cols) if dirn == 0 else (rows <= cols)
        ys = []
        for p in range(heads_per_group // 2):
            sl = slice(p * LANES, (p + 1) * LANES)
            rhs = jnp.concatenate([x_b[:, sl], st_b[:, sl]], axis=0)
            lhs = []
            for hh in range(2):
                h = 2 * p + hh
                cs_col = jnp.broadcast_to(cs[:, h:h + 1], (q, LANES))
                decay = jnp.exp2(jnp.where(mask, cs_col - vt[h:h + 1, :], -jnp.inf))
                m = (gmat * decay).astype(BF16)
                ce = (cc32 * jnp.exp2(cs_col)).astype(BF16)
                lhs.append(jnp.concatenate([m, ce], axis=1))
            res = _dot(jnp.concatenate(lhs, axis=0), rhs)
            ys.append(jnp.where(lane_lo, res[:q], res[q:]))
        st_ref[dirn] = dec_ref[dirn, c][0:1, :] * st + _dot(bt_ref[c], xw_ref[dirn, pl.ds(r0, q), :])
        return jnp.concatenate(ys, axis=1)

    def finish(c, y):
        r0 = pl.multiple_of(c * q, q)
        y = y + yacc_ref[pl.ds(r0, q), :] + dsk * xs_ref[pl.ds(r0, q), :].astype(F32)
        y = y * zs_ref[pl.ds(r0, q), :].astype(F32)
        ms = jnp.mean(y * y, axis=-1, keepdims=True)
        y_ref[pl.ds(r0, q), :] = (y * lax.rsqrt(ms + EPS) * nw).astype(BF16)

    st_ref[...] = h0_ref[...]
    half = nchunks // 2

    def first_half(i, carry, prepare_next=True):
        cf, cb = i, nchunks - 1 - i
        yacc_ref[pl.ds(pl.multiple_of(cf * q, q), q), :] = scan_chunk(cf, 0)
        yacc_ref[pl.ds(pl.multiple_of(cb * q, q), q), :] = scan_chunk(cb, 1)
        if prepare_next:
            prepare(cf + 1)
            prepare(cb - 1)
        return carry

    def second_half(i, carry):
        cf, cb = i, nchunks - 1 - i
        finish(cf, scan_chunk(cf, 0))
        finish(cb, scan_chunk(cb, 1))
        return carry

    prepare(jnp.int32(0))
    prepare(jnp.int32(nchunks - 1))
    lax.fori_loop(0, half - 1, first_half, 0)
    lax.fori_loop(half - 1, half, functools.partial(first_half, prepare_next=False), 0)
    lax.fori_loop(half, nchunks, second_half, 0)


def _ssd_call(xbc, zs, dt, conv_w, conv_b, alog, dsk_e, norm_w, h0, bsz, seqlen, d_ssm, name="ssd_scan"):
    n = N_STATE
    q = CHUNK
    gw = d_ssm // N_GROUPS
    hpg = gw // HEAD_DIM
    n_heads = d_ssm // HEAD_DIM
    xb = d_ssm // n
    cb = xb + N_GROUPS
    nchunks = seqlen // q
    taps = conv_w.shape[0]
    assert nchunks % 2 == 0 and hpg % 8 == 0 and taps // 2 <= HALO
    tl = (jnp.arange(q)[:, None] >= jnp.arange(q)[None, :]).astype(BF16)
    return pl.pallas_call(
        functools.partial(_ssd_kernel, seqlen=seqlen, heads_per_group=hpg, n_heads=n_heads),
        grid=(bsz, N_GROUPS),
        in_specs=[pl.BlockSpec((seqlen, gw), lambda b, g: (b, g)),
                  pl.BlockSpec((seqlen, n), lambda b, g: (b, xb + g)),
                  pl.BlockSpec((seqlen, n), lambda b, g: (b, cb + g)),
                  pl.BlockSpec((seqlen, gw), lambda b, g: (b, g)),
                  pl.BlockSpec((seqlen, LANES), lambda b, g: (b, 0)),
                  pl.BlockSpec((taps, gw), lambda b, g: (0, g)),
                  pl.BlockSpec((taps, n), lambda b, g: (0, xb + g)),
                  pl.BlockSpec((taps, n), lambda b, g: (0, cb + g)),
                  pl.BlockSpec((1, gw), lambda b, g: (0, g)),
                  pl.BlockSpec((1, n), lambda b, g: (0, xb + g)),
                  pl.BlockSpec((1, n), lambda b, g: (0, cb + g)),
                  pl.BlockSpec((1, LANES), lambda b, g: (0, 0)),
                  pl.BlockSpec((1, gw), lambda b, g: (0, g)),
                  pl.BlockSpec((1, gw), lambda b, g: (0, g)),
                  pl.BlockSpec((None, 2, None, n, gw), lambda b, g: (b, 0, g, 0, 0)),
                  pl.BlockSpec((q, q), lambda b, g: (0, 0)),
                  pl.BlockSpec(((taps - 1) * q, q + 2 * HALO), lambda b, g: (0, 0))],
        out_specs=pl.BlockSpec((seqlen, gw), lambda b, g: (b, g)),
        out_shape=jax.ShapeDtypeStruct((bsz * seqlen, d_ssm), BF16),
        scratch_shapes=[pltpu.VMEM((seqlen, gw), BF16),
                        pltpu.VMEM((seqlen, n), BF16),
                        pltpu.VMEM((2, seqlen, LANES), F32),
                        pltpu.VMEM((2, nchunks, 8, q), F32),
                        pltpu.VMEM((2, LANES, q), F32),
                        pltpu.VMEM((2, seqlen, gw), BF16),
                        pltpu.VMEM((2, nchunks, 8, gw), F32),
                        pltpu.VMEM((nchunks, q, q), F32),
                        pltpu.VMEM((nchunks, n, q), BF16),
                        pltpu.VMEM((seqlen, gw), F32),
                        pltpu.VMEM((2, n, gw), F32)],
        compiler_params=_params(("parallel", "arbitrary")),
        name=name,
    )(xbc, xbc, xbc, zs, dt, conv_w, conv_w, conv_w, conv_b, conv_b, conv_b, alog, dsk_e, norm_w, h0, tl,
      _row_shift_matrix(taps, q))


def _ctx_state_kernel(xr_ref, br_ref, dt_ref, cwx_ref, cwb_ref, cbx_ref, cbb_ref, alog_ref, su_ref, shift_ref,
                      h0_ref, *, heads_per_group, n_heads):
    g = pl.program_id(1)
    seqlen, gw = xr_ref.shape
    q = CHUNK
    nchunks = seqlen // q
    xb = jnp.concatenate([_conv_silu_chunk([xr_ref, br_ref], [cwx_ref, cwb_ref], [cbx_ref, cbb_ref], shift_ref,
                                           c, nchunks, q) for c in range(nchunks)], axis=0)
    x, b = xb[:, :gw], xb[:, gw:]
    b_t = b.T.astype(BF16)
    dt = dt_ref[...]
    da = dt * -jnp.exp(alog_ref[...])
    later = _dot_exact_lhs01(su_ref[...], da)
    total = later[0:1, :] + da[0:1, :]
    bwd_lane = lax.broadcasted_iota(jnp.int32, (seqlen, LANES), 1) >= n_heads
    rest = jnp.where(bwd_lane, total - later - da, later)
    wgt = jnp.exp(rest) * dt
    for dirn in range(2):
        e01 = _head_expansion(dirn * n_heads + g * heads_per_group, gw)
        xw = (x * _dot_rhs01_2term(wgt, e01)).astype(BF16)
        h0_ref[dirn] = _dot(b_t, xw)


def _ctx_state_call(xbc, dt, conv_w, conv_b, alog, bsz, seqlen, d_ssm, name="ctx_states"):
    n = N_STATE
    gw = d_ssm // N_GROUPS
    hpg = gw // HEAD_DIM
    n_heads = d_ssm // HEAD_DIM
    xb = d_ssm // n
    su = (jnp.arange(seqlen)[:, None] < jnp.arange(seqlen)[None, :]).astype(BF16)
    taps = conv_w.shape[0]
    assert seqlen % CHUNK == 0
    return pl.pallas_call(
        functools.partial(_ctx_state_kernel, heads_per_group=hpg, n_heads=n_heads),
        grid=(bsz, N_GROUPS),
        in_specs=[pl.BlockSpec((seqlen, gw), lambda b, g: (b, g)),
                  pl.BlockSpec((seqlen, n), lambda b, g: (b, xb + g)),
                  pl.BlockSpec((seqlen, LANES), lambda b, g: (b, 0)),
                  pl.BlockSpec((taps, gw), lambda b, g: (0, g)),
                  pl.BlockSpec((taps, n), lambda b, g: (0, xb + g)),
                  pl.BlockSpec((1, gw), lambda b, g: (0, g)),
                  pl.BlockSpec((1, n), lambda b, g: (0, xb + g)),
                  pl.BlockSpec((1, LANES), lambda b, g: (0, 0)),
                  pl.BlockSpec((seqlen, seqlen), lambda b, g: (0, 0)),
                  pl.BlockSpec(((taps - 1) * CHUNK, CHUNK + 2 * HALO), lambda b, g: (0, 0))],
        out_specs=pl.BlockSpec((None, 2, None, n, gw), lambda b, g: (b, 0, g, 0, 0)),
        out_shape=jax.ShapeDtypeStruct((bsz, 2, N_GROUPS, n, gw), F32),
        compiler_params=_params(("parallel", "arbitrary")),
        name=name,
    )(xbc, xbc, dt, conv_w, conv_w, conv_b, conv_b, alog, su, _row_shift_matrix(taps, CHUNK))


def _tap_conv(src_ref, w_ref, dst_store, n_out, sub, lanes):
    taps = w_ref.shape[0]
    wk = w_ref[:, :, lanes].astype(F32)

    def step(i, carry):
        i0 = i * CONV_STEP
        win = src_ref[pl.ds(i0, taps - 1 + CONV_STEP), sub, lanes].astype(F32)
        for q in range(CONV_STEP):
            dst_store(i0 + q, jnp.sum(win[q:q + taps] * wk, axis=0))
        return carry

    lax.fori_loop(0, n_out // CONV_STEP, step, 0)


def _cconv_kernel(u_ref, wcol_ref, wrow_ref, cb_ref, lw_ref, lb_ref, o_ref,
                  rowpad_ref, col32_ref, colt_ref, ycolt_ref, yrow_ref):
    rows, wcol, d = u_ref.shape
    half = d // 2
    taps = wcol_ref.shape[0]
    pad = taps // 2

    rowpad_ref[0:pad] = jnp.zeros((pad, wcol, half), BF16)
    rowpad_ref[pad + rows:] = jnp.zeros((pad, wcol, half), BF16)
    colt_ref[0:pad] = jnp.zeros((pad, rows, half), BF16)
    colt_ref[pad + wcol:] = jnp.zeros((pad, rows, half), BF16)

    def fill(r, carry):
        rowpad_ref[pad + r] = u_ref[r, :, half:]
        col32_ref[r] = u_ref[r, :, :half].astype(F32)
        return carry

    lax.fori_loop(0, rows, fill, 0)

    def to_column_major(w, carry):
        ycolt_ref[w] = col32_ref[:, w, :]
        return carry

    lax.fori_loop(0, wcol, to_column_major, 0)

    def pack_columns(w, carry):
        colt_ref[pad + w] = ycolt_ref[w].astype(BF16)
        return carry

    lax.fori_loop(0, wcol, pack_columns, 0)

    for j in range(half // LANES):
        lanes = slice(j * LANES, (j + 1) * LANES)
        for s in range(rows // BF16_ROWS):
            sub = slice(s * BF16_ROWS, (s + 1) * BF16_ROWS)

            def store_col(w, val, sub=sub, lanes=lanes):
                ycolt_ref[w, sub, lanes] = val

            _tap_conv(colt_ref, wcol_ref, store_col, wcol, sub, lanes)
        for s in range(wcol // BF16_ROWS):
            sub = slice(s * BF16_ROWS, (s + 1) * BF16_ROWS)

            def store_row(r, val, sub=sub, lanes=lanes):
                yrow_ref[r, sub, lanes] = val

            _tap_conv(rowpad_ref, wrow_ref, store_row, rows, sub, lanes)

    def to_row_major(w, carry):
        col32_ref[:, w, :] = ycolt_ref[w]
        return carry

    lax.fori_loop(0, wcol, to_row_major, 0)

    def norm(r, carry):
        v = jnp.concatenate([col32_ref[r], yrow_ref[r]], axis=1) + cb_ref[...]
        mu = jnp.mean(v, axis=-1, keepdims=True)
        vc = v - mu
        var = jnp.mean(vc * vc, axis=-1, keepdims=True)
        yv = vc * lax.rsqrt(var + EPS) * lw_ref[...] + lb_ref[...]
        o_ref[r] = _silu(yv).astype(BF16)
        return carry

    lax.fori_loop(0, rows, norm, 0, unroll=2)


def _cconv_call(u, cw, cb, lw, lb, bsz, seqlen, name="axial_conv"):
    d = u.shape[1]
    rows = seqlen // GRID_W
    taps = cw.shape[0]
    pad = taps // 2
    half = d // 2
    assert rows % BF16_ROWS == 0 and GRID_W % BF16_ROWS == 0 and rows % CONV_STEP == 0
    wb = jnp.broadcast_to(cw.astype(BF16)[:, None, :], (taps, BF16_ROWS, d))
    grid_block = pl.BlockSpec((None, rows, GRID_W, d), lambda b: (b, 0, 0, 0))
    out = pl.pallas_call(
        _cconv_kernel,
        grid=(bsz,),
        in_specs=[grid_block, _resident((taps, BF16_ROWS, half)), _resident((taps, BF16_ROWS, half)),
                  _resident((1, d)), _resident((1, d)), _resident((1, d))],
        out_specs=grid_block,
        out_shape=jax.ShapeDtypeStruct((bsz, rows, GRID_W, d), BF16),
        scratch_shapes=[pltpu.VMEM((rows + 2 * pad, GRID_W, half), BF16),
                        pltpu.VMEM((rows, GRID_W, half), F32),
                        pltpu.VMEM((GRID_W + 2 * pad, rows, half), BF16),
                        pltpu.VMEM((GRID_W, rows, half), F32),
                        pltpu.VMEM((rows, GRID_W, half), F32)],
        compiler_params=_params(("parallel",)),
        name=name,
    )(u.reshape(bsz, rows, GRID_W, d), wb[:, :, :half], wb[:, :, half:], cb, lw, lb)
    return out.reshape(bsz * seqlen, d)


def _outproj_kernel(x_ref, y_ref, u_ref, mod_ref, wy_ref, wu_ref, o_ref):
    mix = _dot(y_ref[...], wy_ref[...]) + _dot(u_ref[...], wu_ref[...])
    o_ref[...] = x_ref[...] + mod_ref[5:6, :] * mix


def _outproj_call(x, y, u, mod, rows_per_mod, wy, wu, name="out_proj"):
    t, d = x.shape
    tm = _row_tile(t, rows_per_mod, 1024)
    tiles_per_mod = rows_per_mod // tm
    row = lambda i: (i, 0)
    return pl.pallas_call(
        _outproj_kernel,
        grid=(t // tm,),
        in_specs=[pl.BlockSpec((tm, d), row), pl.BlockSpec((tm, y.shape[1]), row), pl.BlockSpec((tm, u.shape[1]), row),
                  pl.BlockSpec((None, N_MOD, d), lambda i: (i // tiles_per_mod, 0, 0)),
                  _resident(wy.shape), _resident(wu.shape)],
        out_specs=pl.BlockSpec((tm, d), row),
        out_shape=jax.ShapeDtypeStruct((t, d), F32),
        compiler_params=_params(("parallel",)),
        name=name,
    )(x, y, u, mod, wy, wu)


def kernel(x, c, ctx, c_ctx, w_mod, b_mod, norm_ffn1, ffn1_gate, ffn1_up, ffn1_down, norm_mix, w_in, ssm_conv_w, ssm_conv_b, dt_bias_fwd, dt_bias_bwd, a_log_fwd, a_log_bwd, ssm_d, ssm_norm_w, cconv_w, cconv_b, cconv_ln_w, cconv_ln_b, w_out, norm_ffn2, ffn2_gate, ffn2_up, ffn2_down, final_norm):
    bsz, seqlen, d = x.shape
    clen = ctx.shape[1]
    depth = w_mod.shape[0]
    d_ssm = ssm_norm_w.shape[1]
    d_conv = cconv_w.shape[2]
    d_xbc = ssm_conv_w.shape[2]
    n_heads = d_ssm // HEAD_DIM
    off_x = d_ssm
    off_dt = off_x + d_xbc
    off_glu = off_dt + 2 * n_heads
    assert seqlen % CHUNK == 0 and seqlen % GRID_W == 0 and 2 * n_heads <= LANES

    def pad_lanes(v):
        return jnp.pad(v, (0, LANES - v.shape[0])).reshape(1, LANES)

    xt = x.reshape(bsz * seqlen, d)
    xc = ctx.reshape(bsz * clen, d)
    assert depth == 1, "the context stream of non-final layers is not implemented"
    c_all = jnp.concatenate([c, c_ctx[None, :], jnp.zeros((-(bsz + 1) % 8, d), F32)], axis=0)
    for i in range(depth):
        last = i == depth - 1
        mod_all = _mod_call(c_all, w_mod[i], b_mod[i].reshape(1, -1))
        mod = mod_all[:bsz].reshape(bsz, N_MOD, d)
        mod_c = mod_all[bsz:bsz + 1].reshape(1, N_MOD, d)

        wg1, wu1, wd1 = ffn1_gate[i].astype(BF16), ffn1_up[i].astype(BF16), ffn1_down[i].astype(BF16)
        nw1 = norm_ffn1[i].reshape(1, d)
        xt = _ffn_call(xt, mod, seqlen, 0, nw1, wg1, wu1, wd1, name="ffn1")
        xc = _ffn_call(xc, mod_c, bsz * clen, 0, nw1, wg1, wu1, wd1, name="ffn1_ctx")

        wi = w_in[i]
        w_cat = jnp.concatenate(
            [wi[:, :off_dt], jnp.pad(wi[:, off_dt:off_glu], ((0, 0), (0, LANES - 2 * n_heads))), wi[:, off_glu:]],
            axis=1).astype(BF16)
        dtb = pad_lanes(jnp.concatenate([dt_bias_fwd[i], dt_bias_bwd[i]]))
        alog = pad_lanes(jnp.concatenate([a_log_fwd[i], a_log_bwd[i]]))
        nwm = norm_mix[i].reshape(1, d)
        conv_b = ssm_conv_b[i].reshape(1, -1)

        zs, xbc, dt, u = _inproj_call(xt, mod, seqlen, nwm, w_cat, dtb, d_ssm, d_xbc, d_conv, False, name="in_proj")
        xbc_c, dt_c = _inproj_call(xc, mod_c, bsz * clen, nwm, w_cat, dtb, d_ssm, d_xbc, d_conv, True,
                                   name="in_proj_ctx")
        h0 = _ctx_state_call(xbc_c, dt_c, ssm_conv_w[i], conv_b, alog, bsz, clen, d_ssm)

        dsk_e = jnp.repeat(ssm_d[i], HEAD_DIM).reshape(1, d_ssm)
        y = _ssd_call(xbc, zs, dt, ssm_conv_w[i], conv_b, alog, dsk_e, ssm_norm_w[i].reshape(1, d_ssm), h0,
                      bsz, seqlen, d_ssm)
        uc = _cconv_call(u, cconv_w[i], cconv_b[i].reshape(1, -1), cconv_ln_w[i].reshape(1, -1),
                         cconv_ln_b[i].reshape(1, -1), bsz, seqlen)
        wo = w_out[i].astype(BF16)
        xt = _outproj_call(xt, y, uc, mod, seqlen, wo[:d_ssm], wo[d_ssm:])
        xt = _ffn_call(xt, mod, seqlen, 6, norm_ffn2[i].reshape(1, d), ffn2_gate[i].astype(BF16),
                       ffn2_up[i].astype(BF16), ffn2_down[i].astype(BF16),
                       final_w=final_norm.reshape(1, d) if last else None, name="ffn2")
    return xt.reshape(bsz, seqlen, d)
```
